```python
import math
import jax, jax.numpy as jnp
from jax import lax
import numpy as np

D_MODEL = 2048
BATCH = 4
SEQ = 2048
DEPTH = 4

DA_HEADS = 4
DA_QK_DIM = 64
DA_V_DIM = 128
Q_BLOCK = 128
HG_HEADS = 6
HG_K_DIM = 128
HG_V_DIM = 128
HG_CHUNK = 16
NSA_HEADS = 6
NSA_KV_HEADS = 2
NSA_GROUP = NSA_HEADS // NSA_KV_HEADS
NSA_DIM = 128
CMP_LEN = 32
CMP_STRIDE = 16
CMP_HIDDEN = 256
SLC_BLOCK = 64
SLC_TOPK = 16
SLC_Q_BLOCK = 64
WINDOW = 512
WIN_Q_BLOCK = 128
FORCE_BONUS = 1.0e4
D_A = DA_HEADS * DA_V_DIM
D_B = HG_HEADS * HG_V_DIM
D_C = NSA_HEADS * NSA_DIM
D_MIX = D_A + D_B + D_C
D_FF = ((8 * D_MODEL // 3 + 255) // 256) * 256
IN_SIZES = (
    DA_HEADS * 2 * DA_QK_DIM, DA_HEADS * 2 * DA_QK_DIM, D_A,
    HG_HEADS * HG_K_DIM, HG_HEADS * HG_K_DIM, D_B, D_B,
    D_C,
    NSA_KV_HEADS * NSA_DIM, NSA_KV_HEADS * NSA_DIM,
    NSA_KV_HEADS * NSA_DIM, NSA_KV_HEADS * NSA_DIM,
    NSA_KV_HEADS * NSA_DIM, NSA_KV_HEADS * NSA_DIM,
    NSA_HEADS * 3,
)
D_IN = sum(IN_SIZES)

kernel_name = "hymba_diffattn_hgrn2_nsa_trunk"


def rmsnorm(x, g, eps=1e-6):
    xf = x.astype(jnp.float32)
    y = xf * lax.rsqrt(jnp.mean(xf * xf, axis=-1, keepdims=True) + eps)
    return (y * g.astype(jnp.float32)).astype(x.dtype)


def masked_softmax(s, mask):
    s = jnp.where(mask, s.astype(jnp.float32), -jnp.inf)
    m = jnp.max(s, axis=-1, keepdims=True)
    m = jnp.where(jnp.isfinite(m), m, 0.0)
    p = jnp.exp(s - m)
    return p / jnp.maximum(jnp.sum(p, axis=-1, keepdims=True), 1e-30)


def diff_attention(q, k, v, lam):
    B, T = q.shape[:2]
    nblk = T // Q_BLOCK
    scale = DA_QK_DIM ** -0.5
    qb = jnp.moveaxis(q.reshape(B, nblk, Q_BLOCK, DA_HEADS, 2, DA_QK_DIM), 1, 0)
    kpos = jnp.arange(T)

    def block(args):
        qi, i = args
        s = jnp.einsum('bqhcd,bkhcd->bhcqk', qi, k) * scale
        qpos = i * Q_BLOCK + jnp.arange(Q_BLOCK)
        p = masked_softmax(s, kpos[None, :] <= qpos[:, None])
        pd = p[:, :, 0] - lam * p[:, :, 1]
        return jnp.einsum('bhqk,bkhd->bqhd', pd.astype(v.dtype), v)

    o = lax.map(block, (qb, jnp.arange(nblk)))
    return jnp.moveaxis(o, 0, 1).reshape(B, T, DA_HEADS, DA_V_DIM)


def hgrn2_mixer(f_logit, q, i, lb):
    B, T = f_logit.shape[:2]
    n = T // HG_CHUNK
    lb = lb.astype(jnp.float32).reshape(HG_HEADS, HG_K_DIM)
    logf = jnp.logaddexp(jnp.log(lb), jnp.log1p(-lb) + jax.nn.log_sigmoid(f_logit.astype(jnp.float32)))
    key = -jnp.expm1(logf)

    def to_chunks(a):
        return a.astype(jnp.float32).reshape(B, n, HG_CHUNK, HG_HEADS, -1).transpose(0, 3, 1, 2, 4)

    logf_c, q_c, k_c, v_c = to_chunks(logf), to_chunks(q), to_chunks(key), to_chunks(i)
    cum = jnp.cumsum(logf_c, axis=3)
    causal = jnp.tril(jnp.ones((HG_CHUNK, HG_CHUNK), bool))[:, :, None]
    diff = cum[:, :, :, :, None, :] - cum[:, :, :, None, :, :]
    decay = jnp.where(causal, jnp.exp(jnp.where(causal, diff, 0.0)), 0.0)
    scores = jnp.einsum('bhntk,bhnsk,bhntsk->bhnts', q_c, k_c, decay)
    o_intra = jnp.einsum('bhnts,bhnsv->bhntv', scores, v_c)
    last = cum[:, :, :, -1:, :]
    d_state = jnp.einsum('bhnsk,bhnsv->bhnkv', k_c * jnp.exp(last - cum), v_c)
    chunk_decay = jnp.exp(last[:, :, :, 0, :])

    def step(S, inp):
        dec, dS = inp
        return dec[..., None] * S + dS, S

    S0 = jnp.zeros((B, HG_HEADS, HG_K_DIM, HG_V_DIM), jnp.float32)
    _, S_prev = lax.scan(step, S0, (jnp.moveaxis(chunk_decay, 2, 0), jnp.moveaxis(d_state, 2, 0)))
    S_prev = jnp.moveaxis(S_prev, 0, 2)
    o_inter = jnp.einsum('bhntk,bhnkv->bhntv', q_c * jnp.exp(cum), S_prev)
    return (o_intra + o_inter).transpose(0, 2, 3, 1, 4).reshape(B, T, HG_HEADS, HG_V_DIM)


def compress_blocks(kv, cidx, pe, w1, w2):
    B = kv.shape[0]
    n_cmp = cidx.shape[0]
    blocks = kv[:, cidx] + pe[None, None, :, None, :]
    flat = blocks.transpose(0, 3, 1, 2, 4).reshape(B, NSA_KV_HEADS, n_cmp, CMP_LEN * NSA_DIM)
    return jax.nn.gelu(flat @ w1) @ w2


def nsa_mixer(q, k_cmp, v_cmp, k_slc, v_slc, k_win, v_win, gate_logit, pe_k, pe_v, ck_w1, ck_w2, cv_w1, cv_w2):
    B, T = q.shape[:2]
    G, J, D = NSA_KV_HEADS, NSA_GROUP, NSA_DIM
    scale = D ** -0.5
    q = q.reshape(B, T, G, J, D)
    split_kv = lambda a: a.reshape(B, T, G, D)
    k_cmp, v_cmp, k_slc, v_slc, k_win, v_win = map(split_kv, (k_cmp, v_cmp, k_slc, v_slc, k_win, v_win))
    tpos = jnp.arange(T)

    n_cmp = (T - CMP_LEN) // CMP_STRIDE + 1
    cidx = np.arange(n_cmp)[:, None] * CMP_STRIDE + np.arange(CMP_LEN)[None, :]
    kc = compress_blocks(k_cmp, cidx, pe_k, ck_w1, ck_w2)
    vc = compress_blocks(v_cmp, cidx, pe_v, cv_w1, cv_w2)
    s_c = jnp.einsum('btgjd,bgnd->bgjtn', q, kc) * scale
    cmp_mask = jnp.asarray(cidx[:, -1])[None, :] <= tpos[:, None]
    p_c = masked_softmax(s_c, cmp_mask)
    o_cmp = jnp.einsum('bgjtn,bgnd->btgjd', p_c.astype(vc.dtype), vc)

    n_sel = T // SLC_BLOCK
    c_start = np.arange(n_cmp) * CMP_STRIDE
    s_start = np.arange(n_sel) * SLC_BLOCK
    overlap = ((c_start[:, None] < s_start[None, :] + SLC_BLOCK)
               & (c_start[:, None] + CMP_LEN > s_start[None, :])).astype(np.float32)
    imp = jnp.einsum('bgjtn,ns->bgts', p_c, jnp.asarray(overlap))
    blk = jnp.arange(n_sel)[None, :]
    cur = (tpos // SLC_BLOCK)[:, None]
    forced = (blk == 0) | (blk == cur) | (blk == cur - 1)
    valid = blk * SLC_BLOCK <= tpos[:, None]
    score = jnp.where(valid, imp + jnp.where(forced, FORCE_BONUS, 0.0), -jnp.inf)
    top = min(SLC_TOPK, n_sel)
    _, sel_idx = lax.top_k(score, top)

    kblk = k_slc.reshape(B, n_sel, SLC_BLOCK, G, D).transpose(0, 3, 1, 2, 4)
    vblk = v_slc.reshape(B, n_sel, SLC_BLOCK, G, D).transpose(0, 3, 1, 2, 4)
    nq = T // SLC_Q_BLOCK
    qc = jnp.moveaxis(q.reshape(B, nq, SLC_Q_BLOCK, G, J, D), 1, 0)
    idxc = jnp.moveaxis(sel_idx.reshape(B, G, nq, SLC_Q_BLOCK, top), 2, 0)
    bi = jnp.arange(B)[:, None, None, None]
    gi = jnp.arange(G)[None, :, None, None]

    def sel_block(args):
        qi, ii, c = args
        ksel = kblk[bi, gi, ii]
        vsel = vblk[bi, gi, ii]
        s = jnp.einsum('bqgjd,bgqnld->bgjqnl', qi, ksel) * scale
        s = s.reshape(B, G, J, SLC_Q_BLOCK, top * SLC_BLOCK)
        kpos = ii[..., None] * SLC_BLOCK + jnp.arange(SLC_BLOCK)
        qpos = c * SLC_Q_BLOCK + jnp.arange(SLC_Q_BLOCK)
        mask = (kpos <= qpos[None, None, :, None, None]).reshape(B, G, 1, SLC_Q_BLOCK, top * SLC_BLOCK)
        p = masked_softmax(s, mask)
        return jnp.einsum('bgjqm,bgqmd->bqgjd', p.astype(vsel.dtype),
                          vsel.reshape(B, G, SLC_Q_BLOCK, top * SLC_BLOCK, D))

    o_slc = lax.map(sel_block, (qc, idxc, jnp.arange(nq)))
    o_slc = jnp.moveaxis(o_slc, 0, 1).reshape(B, T, G, J, D)

    nw = T // WIN_Q_BLOCK
    span = WINDOW + WIN_Q_BLOCK
    widx = np.arange(nw)[:, None] * WIN_Q_BLOCK + np.arange(span)[None, :]
    pad = ((0, 0), (WINDOW, 0), (0, 0), (0, 0))
    kwb = jnp.pad(k_win, pad)[:, widx]
    vwb = jnp.pad(v_win, pad)[:, widx]
    qw = q.reshape(B, nw, WIN_Q_BLOCK, G, J, D)
    s_w = jnp.einsum('bnqgjd,bnkgd->bgjnqk', qw, kwb) * scale
    kpos_w = widx - WINDOW
    qpos_w = np.arange(nw)[:, None] * WIN_Q_BLOCK + np.arange(WIN_Q_BLOCK)[None, :]
    dist = qpos_w[:, :, None] - kpos_w[:, None, :]
    p_w = masked_softmax(s_w, jnp.asarray((dist >= 0) & (dist < WINDOW)))
    o_win = jnp.einsum('bgjnqk,bnkgd->bnqgjd', p_w.astype(vwb.dtype), vwb).reshape(B, T, G, J, D)

    g = jax.nn.sigmoid(gate_logit.reshape(B, T, G, J, 3))
    o = g[..., 0:1] * o_cmp + g[..., 1:2] * o_slc + g[..., 2:3] * o_win
    return o.reshape(B, T, D_C)


def setup_inputs(seed: int = 0) -> dict:
    key = jax.random.key(seed)
    ks = jax.random.split(key, 24)
    f32 = jnp.float32

    def nrm(k, shape, scale):
        return jax.random.normal(k, shape, f32) * scale

    def gain(k, shape):
        return 1.0 + 0.01 * jax.random.normal(k, shape, f32)

    cmp_in = CMP_LEN * NSA_DIM
    return {
        "x": nrm(ks[0], (BATCH, SEQ, D_MODEL), 1.0),
        "attn_norm": gain(ks[1], (DEPTH, D_MODEL)),
        "w_in": nrm(ks[2], (DEPTH, D_MODEL, D_IN), D_MODEL ** -0.5),
        "da_lam_q1": nrm(ks[3], (DEPTH, DA_QK_DIM), 0.1),
        "da_lam_k1": nrm(ks[4], (DEPTH, DA_QK_DIM), 0.1),
        "da_lam_q2": nrm(ks[5], (DEPTH, DA_QK_DIM), 0.1),
        "da_lam_k2": nrm(ks[6], (DEPTH, DA_QK_DIM), 0.1),
        "da_norm": gain(ks[7], (DEPTH, DA_HEADS, DA_V_DIM)),
        "hg_gamma": nrm(ks[8], (DEPTH, HG_HEADS * HG_K_DIM), 0.5),
        "hg_norm": gain(ks[9], (DEPTH, HG_HEADS, HG_V_DIM)),
        "nsa_pe_k": nrm(ks[10], (DEPTH, CMP_LEN, NSA_DIM), 0.1),
        "nsa_pe_v": nrm(ks[11], (DEPTH, CMP_LEN, NSA_DIM), 0.1),
        "nsa_ck_w1": nrm(ks[12], (DEPTH, cmp_in, CMP_HIDDEN), cmp_in ** -0.5),
        "nsa_ck_w2": nrm(ks[13], (DEPTH, CMP_HIDDEN, NSA_DIM), CMP_HIDDEN ** -0.5),
        "nsa_cv_w1": nrm(ks[14], (DEPTH, cmp_in, CMP_HIDDEN), cmp_in ** -0.5),
        "nsa_cv_w2": nrm(ks[15], (DEPTH, CMP_HIDDEN, NSA_DIM), CMP_HIDDEN ** -0.5),
        "w_out": nrm(ks[16], (DEPTH, D_MIX, D_MODEL), D_MIX ** -0.5),
        "ffn_norm": gain(ks[17], (DEPTH, D_MODEL)),
        "w_gate": nrm(ks[18], (DEPTH, D_MODEL, D_FF), D_MODEL ** -0.5),
        "w_up": nrm(ks[19], (DEPTH, D_MODEL, D_FF), D_MODEL ** -0.5),
        "w_down": nrm(ks[20], (DEPTH, D_FF, D_MODEL), D_FF ** -0.5),
        "final_norm": gain(ks[21], (D_MODEL,)),
    }


def reference(x, attn_norm, w_in, da_lam_q1, da_lam_k1, da_lam_q2, da_lam_k2, da_norm, hg_gamma, hg_norm,
              nsa_pe_k, nsa_pe_v, nsa_ck_w1, nsa_ck_w2, nsa_cv_w1, nsa_cv_w2, w_out, ffn_norm,
              w_gate, w_up, w_down, final_norm):
    B, T, _ = x.shape
    offsets = np.cumsum(IN_SIZES)[:-1].tolist()
    lbs = jnp.cumsum(jax.nn.softmax(hg_gamma.astype(jnp.float32), axis=0), axis=0)
    lbs = lbs - lbs[0:1]
    for l in range(DEPTH):
        h = rmsnorm(x, attn_norm[l])
        (a_q, a_k, a_v, b_f, b_q, b_i, b_g, c_q, c_kc, c_vc, c_ks, c_vs, c_kw, c_vw, c_g) = \
            jnp.split(h @ w_in[l], offsets, axis=-1)

        lam_init = 0.8 - 0.6 * math.exp(-0.3 * l)
        lam = (jnp.exp(jnp.sum(da_lam_q1[l].astype(jnp.float32) * da_lam_k1[l].astype(jnp.float32)))
               - jnp.exp(jnp.sum(da_lam_q2[l].astype(jnp.float32) * da_lam_k2[l].astype(jnp.float32)))
               + lam_init)
        oa = diff_attention(a_q.reshape(B, T, DA_HEADS, 2, DA_QK_DIM),
                            a_k.reshape(B, T, DA_HEADS, 2, DA_QK_DIM),
                            a_v.reshape(B, T, DA_HEADS, DA_V_DIM), lam)
        ya = (rmsnorm(oa, da_norm[l]) * (1.0 - lam_init)).reshape(B, T, D_A).astype(x.dtype)

        ob = hgrn2_mixer(b_f.reshape(B, T, HG_HEADS, HG_K_DIM), b_q.reshape(B, T, HG_HEADS, HG_K_DIM),
                         b_i.reshape(B, T, HG_HEADS, HG_V_DIM), lbs[l])
        yb = (rmsnorm(ob, hg_norm[l]) * jax.nn.silu(b_g.reshape(B, T, HG_HEADS, HG_V_DIM).astype(jnp.float32)))
        yb = yb.reshape(B, T, D_B).astype(x.dtype)

        yc = nsa_mixer(c_q, c_kc, c_vc, c_ks, c_vs, c_kw, c_vw, c_g, nsa_pe_k[l], nsa_pe_v[l],
                       nsa_ck_w1[l], nsa_ck_w2[l], nsa_cv_w1[l], nsa_cv_w2[l]).astype(x.dtype)

        x = x + jnp.concatenate([ya, yb, yc], axis=-1) @ w_out[l]
        h = rmsnorm(x, ffn_norm[l])
        x = x + (jax.nn.silu(h @ w_gate[l]) * (h @ w_up[l])) @ w_down[l]
    return rmsnorm(x, final_norm)
```

```python
import functools
import math

import jax
import jax.numpy as jnp
from jax import lax
from jax.experimental import pallas as pl
from jax.experimental.pallas import tpu as pltpu

F32 = jnp.float32
BF16 = jnp.bfloat16

D_MODEL = 2048
DA_HEADS = 4
DA_QK_DIM = 64
DA_V_DIM = 128
HG_HEADS = 6
HG_DIM = 128
HG_CHUNK = 16
NSA_HEADS = 6
NSA_KV_HEADS = 2
NSA_GROUP = NSA_HEADS // NSA_KV_HEADS
NSA_DIM = 128
CMP_LEN = 32
CMP_STRIDE = 16
CMP_HIDDEN = 256
SLC_BLOCK = 64
SLC_TOPK = 16
WINDOW = 512
FORCE_BONUS = 1.0e4
D_A = DA_HEADS * DA_V_DIM
D_B = HG_HEADS * HG_DIM
D_C = NSA_HEADS * NSA_DIM
D_FF = ((8 * D_MODEL // 3 + 255) // 256) * 256
D_IN = 3 * D_A + 4 * D_B + D_C + 6 * NSA_KV_HEADS * NSA_DIM + NSA_HEADS * 3

LANE = 128
D_IN_PAD = 7168
CB_AQ, CB_AK, CB_AV = 0, 4, 8
CB_BF, CB_BQ, CB_BI, CB_BG = 12, 18, 24, 30
CB_CQ = 36
CB_KC, CB_VC, CB_KS, CB_VS, CB_KW, CB_VW = 42, 44, 46, 48, 50, 52
CB_CG = 54

EPS = 1e-6
NEG = -1e30
VMEM_LIMIT = 56 * 1024 * 1024

_NT = (((1,), (1,)), ((), ()))


def _cparams(sem):
    return pltpu.CompilerParams(dimension_semantics=sem, vmem_limit_bytes=VMEM_LIMIT)


def _rms_rows(x, gain):
    return x * lax.rsqrt(jnp.mean(x * x, axis=-1, keepdims=True) + EPS) * gain


def _norm_matmul_kernel(x_ref, g_ref, w_ref, o_ref, h_scr):
    @pl.when(pl.program_id(1) == 0)
    def _():
        h_scr[...] = _rms_rows(x_ref[...], g_ref[...]).astype(BF16)

    o_ref[...] = jnp.dot(h_scr[...], w_ref[...], preferred_element_type=F32)


def _norm_matmul(x, gain, w, tm=1024, tn=512):
    m, k = x.shape
    n = w.shape[1]
    return pl.pallas_call(
        _norm_matmul_kernel,
        grid=(m // tm, n // tn),
        in_specs=[pl.BlockSpec((tm, k), lambda i, j: (i, 0)),
                  pl.BlockSpec((1, k), lambda i, j: (0, 0)),
                  pl.BlockSpec((k, tn), lambda i, j: (0, j))],
        out_specs=pl.BlockSpec((tm, tn), lambda i, j: (i, j)),
        out_shape=jax.ShapeDtypeStruct((m, n), F32),
        scratch_shapes=[pltpu.VMEM((tm, k), BF16)],
        compiler_params=_cparams(("parallel", "arbitrary")),
        name="norm_in_proj",
    )(x, gain.reshape(1, k), w)


def _ffn_up_kernel(x_ref, g_ref, wg_ref, wu_ref, o_ref, h_scr):
    @pl.when(pl.program_id(1) == 0)
    def _():
        h_scr[...] = _rms_rows(x_ref[...], g_ref[...]).astype(BF16)

    h = h_scr[...]
    a = jnp.dot(h, wg_ref[...], preferred_element_type=F32)
    b = jnp.dot(h, wu_ref[...], preferred_element_type=F32)
    o_ref[...] = (a * jax.nn.sigmoid(a) * b).astype(BF16)


def _ffn_up(x, gain, wg, wu, tm=1024, tn=512):
    m, k = x.shape
    n = wg.shape[1]
    return pl.pallas_call(
        _ffn_up_kernel,
        grid=(m // tm, n // tn),
        in_specs=[pl.BlockSpec((tm, k), lambda i, j: (i, 0)),
                  pl.BlockSpec((1, k), lambda i, j: (0, 0)),
                  pl.BlockSpec((k, tn), lambda i, j: (0, j)),
                  pl.BlockSpec((k, tn), lambda i, j: (0, j))],
        out_specs=pl.BlockSpec((tm, tn), lambda i, j: (i, j)),
        out_shape=jax.ShapeDtypeStruct((m, n), BF16),
        scratch_shapes=[pltpu.VMEM((tm, k), BF16)],
        compiler_params=_cparams(("parallel", "arbitrary")),
        name="ffn_up",
    )(x, gain.reshape(1, k), wg, wu)


def _matmul_res_kernel(a_ref, w_ref, r_ref, o_ref):
    o_ref[...] = r_ref[...] + jnp.dot(a_ref[...], w_ref[...], preferred_element_type=F32)


def _matmul_res(a, w, res, tm=512, tn=512):
    m, k = a.shape
    n = w.shape[1]
    return pl.pallas_call(
        _matmul_res_kernel,
        grid=(m // tm, n // tn),
        in_specs=[pl.BlockSpec((tm, k), lambda i, j: (i, 0)),
                  pl.BlockSpec((k, tn), lambda i, j: (0, j)),
                  pl.BlockSpec((tm, tn), lambda i, j: (i, j))],
        out_specs=pl.BlockSpec((tm, tn), lambda i, j: (i, j)),
        out_shape=jax.ShapeDtypeStruct((m, n), F32),
        compiler_params=_cparams(("parallel", "arbitrary")),
        name="ffn_down",
    )(a, w, res)


def _out_proj_kernel(ya_ref, yb_ref, yc_ref, w_ref, r_ref, o_ref):
    acc = jnp.dot(ya_ref[...], w_ref[0:D_A, :], preferred_element_type=F32)
    acc += jnp.dot(yb_ref[...], w_ref[D_A:D_A + D_B, :], preferred_element_type=F32)
    acc += jnp.dot(yc_ref[...], w_ref[D_A + D_B:, :], preferred_element_type=F32)
    o_ref[...] = r_ref[...] + acc


def _out_proj(ya, yb, yc, w, res, tm=1024, tn=512):
    m = ya.shape[0]
    k, n = w.shape
    return pl.pallas_call(
        _out_proj_kernel,
        grid=(m // tm, n // tn),
        in_specs=[pl.BlockSpec((tm, D_A), lambda i, j: (i, 0)),
                  pl.BlockSpec((tm, D_B), lambda i, j: (i, 0)),
                  pl.BlockSpec((tm, D_C), lambda i, j: (i, 0)),
                  pl.BlockSpec((k, tn), lambda i, j: (0, j)),
                  pl.BlockSpec((tm, tn), lambda i, j: (i, j))],
        out_specs=pl.BlockSpec((tm, tn), lambda i, j: (i, j)),
        out_shape=jax.ShapeDtypeStruct((m, n), F32),
        compiler_params=_cparams(("parallel", "arbitrary")),
        name="out_proj",
    )(ya, yb, yc, w, res)


def _final_norm_kernel(x_ref, g_ref, o_ref):
    o_ref[...] = _rms_rows(x_ref[...], g_ref[...])


def _final_norm(x, gain, tm=512):
    m, k = x.shape
    return pl.pallas_call(
        _final_norm_kernel,
        grid=(m // tm,),
        in_specs=[pl.BlockSpec((tm, k), lambda i: (i, 0)),
                  pl.BlockSpec((1, k), lambda i: (0, 0))],
        out_specs=pl.BlockSpec((tm, k), lambda i: (i, 0)),
        out_shape=jax.ShapeDtypeStruct((m, k), F32),
        compiler_params=_cparams(("parallel",)),
        name="final_norm",
    )(x, gain.reshape(1, k))


def _flash_step(q, k, v, bias, carry):
    m, l, acc = carry
    s = lax.dot_general(q, k, _NT, preferred_element_type=F32) + bias
    m_new = jnp.maximum(m, jnp.max(s, axis=-1, keepdims=True))
    alpha = jnp.exp(m - m_new)
    p = jnp.exp(s - m_new)
    l = alpha * l + jnp.sum(p, axis=-1, keepdims=True)
    acc = alpha * acc + jnp.dot(p.astype(BF16), v, preferred_element_type=F32)
    return m_new, l, acc


def _flash_init(rows, dv):
    return (jnp.full((rows, 1), NEG, F32), jnp.zeros((rows, 1), F32), jnp.zeros((rows, dv), F32))


def _flash_out(carry):
    _, l, acc = carry
    return acc / jnp.maximum(l, 1e-30)


def _diff_attn_kernel(q_ref, k_ref, v_ref, lq1_ref, lk1_ref, lq2_ref, lk2_ref, gn_ref, o_ref,
                      *, lam_init, tq):
    h = pl.program_id(1)
    i = pl.program_id(2)
    scale = DA_QK_DIM ** -0.5
    q = (q_ref[...] * scale).astype(BF16)
    q0 = q[:, :DA_QK_DIM]
    q1 = q[:, DA_QK_DIM:]
    qpos = i * tq + lax.broadcasted_iota(jnp.int32, (tq, tq), 0)
    kiota = lax.broadcasted_iota(jnp.int32, (tq, tq), 1)

    def body(kb, carry):
        c0, c1 = carry
        k0 = pl.multiple_of(kb * tq, tq)
        k = k_ref[pl.ds(k0, tq), :].astype(BF16)
        v = v_ref[pl.ds(k0, tq), :].astype(BF16)
        bias = jnp.where(k0 + kiota <= qpos, 0.0, NEG)
        c0 = _flash_step(q0, k[:, :DA_QK_DIM], v, bias, c0)
        c1 = _flash_step(q1, k[:, DA_QK_DIM:], v, bias, c1)
        return c0, c1

    c0, c1 = lax.fori_loop(0, i + 1, body, (_flash_init(tq, DA_V_DIM), _flash_init(tq, DA_V_DIM)))
    lam = (jnp.exp(jnp.sum(lq1_ref[...] * lk1_ref[...], axis=-1, keepdims=True))
           - jnp.exp(jnp.sum(lq2_ref[...] * lk2_ref[...], axis=-1, keepdims=True)) + lam_init)
    o = _flash_out(c0) - lam * _flash_out(c1)
    y = _rms_rows(o, gn_ref[pl.ds(h, 1), :]) * (1.0 - lam_init)
    o_ref[...] = y.astype(BF16)


def _diff_attn(proj, lq1, lk1, lq2, lk2, gn, layer, batch, seq, tq=256):
    nq = seq // tq
    lam_init = 0.8 - 0.6 * math.exp(-0.3 * layer)
    vec = pl.BlockSpec((1, DA_QK_DIM), lambda b, h, i: (0, 0))
    return pl.pallas_call(
        functools.partial(_diff_attn_kernel, lam_init=lam_init, tq=tq),
        grid=(batch, DA_HEADS, nq),
        in_specs=[pl.BlockSpec((tq, LANE), lambda b, h, i: (b * nq + i, CB_AQ + h)),
                  pl.BlockSpec((seq, LANE), lambda b, h, i: (b, CB_AK + h)),
                  pl.BlockSpec((seq, LANE), lambda b, h, i: (b, CB_AV + h)),
                  vec, vec, vec, vec,
                  pl.BlockSpec((DA_HEADS, DA_V_DIM), lambda b, h, i: (0, 0))],
        out_specs=pl.BlockSpec((tq, LANE), lambda b, h, i: (b * nq + i, h)),
        out_shape=jax.ShapeDtypeStruct((batch * seq, D_A), BF16),
        compiler_params=_cparams(("parallel", "parallel", "arbitrary")),
        name="diff_attn",
    )(proj, proj, proj, lq1.reshape(1, -1), lk1.reshape(1, -1), lq2.reshape(1, -1), lk2.reshape(1, -1), gn)


def _split3_dot(mat01, x):
    hi = x.astype(BF16)
    r1 = x - hi.astype(F32)
    mid = r1.astype(BF16)
    lo = (r1 - mid.astype(F32)).astype(BF16)
    return (jnp.dot(mat01, hi, preferred_element_type=F32)
            + jnp.dot(mat01, mid, preferred_element_type=F32)
            + jnp.dot(mat01, lo, preferred_element_type=F32))


def _hgrn_kernel(f_ref, q_ref, i_ref, g_ref, gam_ref, gn_ref, o_ref,
                 st_scr, cum_scr, key_scr, qe_scr, ke_scr, dec_scr, *, layer, tb):
    h = pl.program_id(1)
    c16 = HG_CHUNK

    @pl.when(pl.program_id(2) == 0)
    def _():
        st_scr[...] = jnp.zeros_like(st_scr)

    z = f_ref[...]
    soft = jnp.log(1.0 + jnp.exp(-jnp.abs(z)))
    logsig = jnp.minimum(z, 0.0) - soft
    logsig_neg = jnp.minimum(-z, 0.0) - soft
    if layer == 0:
        logf = logsig
        key = jnp.exp(logsig_neg)
    else:
        gam = gam_ref[...]
        e = jnp.exp(gam - jnp.max(gam, axis=0, keepdims=True))
        lb = jnp.sum(e[1:layer + 1], axis=0, keepdims=True) / jnp.sum(e, axis=0, keepdims=True)
        a = jnp.log(lb)
        b = jnp.log(1.0 - lb) + logsig
        logf = jnp.maximum(a, b) + jnp.log(1.0 + jnp.exp(-jnp.abs(a - b)))
        key = (1.0 - lb) * jnp.exp(logsig_neg)

    row = lax.broadcasted_iota(jnp.int32, (tb, tb), 0)
    col = lax.broadcasted_iota(jnp.int32, (tb, tb), 1)
    same = (row // c16) == (col // c16)
    tril = jnp.where(same & (col <= row), 1.0, 0.0).astype(BF16)
    ones = jnp.where(same, 1.0, 0.0).astype(BF16)
    cum = _split3_dot(tril, logf)
    tot = _split3_dot(ones, logf)
    q = q_ref[...]
    cum_scr[...] = cum
    key_scr[...] = key
    qe_scr[...] = q * jnp.exp(cum)
    ke_scr[...] = key * jnp.exp(tot - cum)
    dec_scr[...] = jnp.exp(tot)
    v_t = i_ref[...].T
    lane_chunk = lax.broadcasted_iota(jnp.int32, (HG_DIM, tb), 1) // c16
    row16 = lax.broadcasted_iota(jnp.int32, (c16, 1), 0)
    gain = gn_ref[pl.ds(h, 1), :]

    def chunk(c, carry):
        r0 = pl.multiple_of(c * c16, c16)
        rows = pl.ds(r0, c16)
        st = st_scr[...]
        o = lax.dot_general(qe_scr[rows, :].astype(BF16), st.astype(BF16), _NT, preferred_element_type=F32)
        cumc = cum_scr[rows, :]
        qc = q_ref[rows, :]
        kc = key_scr[rows, :]
        vc = i_ref[rows, :]
        for s in range(c16):
            w = jnp.exp(jnp.minimum(cumc - cumc[s:s + 1, :], 0.0))
            sc = jnp.sum(qc * w * kc[s:s + 1, :], axis=-1, keepdims=True)
            o = o + jnp.where(row16 >= s, sc, 0.0) * vc[s:s + 1, :]
        gt = g_ref[rows, :]
        o_ref[rows, :] = (_rms_rows(o, gain) * (gt * jax.nn.sigmoid(gt))).astype(BF16)
        vm = jnp.where(lane_chunk == c, v_t, 0.0).astype(BF16)
        st_scr[...] = st * dec_scr[pl.ds(r0, 1), :] + jnp.dot(vm, ke_scr[...].astype(BF16),
                                                             preferred_element_type=F32)
        return carry

    lax.fori_loop(0, tb // c16, chunk, 0)


def _hgrn(proj, gamma, gn, layer, batch, seq, tb=128):
    nt = seq // tb
    depth = gamma.shape[0]

    def blk(cb):
        return pl.BlockSpec((tb, LANE), lambda b, h, t: (b * nt + t, cb + h))

    scr = pltpu.VMEM((tb, HG_DIM), F32)
    return pl.pallas_call(
        functools.partial(_hgrn_kernel, layer=layer, tb=tb),
        grid=(batch, HG_HEADS, nt),
        in_specs=[blk(CB_BF), blk(CB_BQ), blk(CB_BI), blk(CB_BG),
                  pl.BlockSpec((depth, LANE), lambda b, h, t: (0, h)),
                  pl.BlockSpec((HG_HEADS, HG_DIM), lambda b, h, t: (0, 0))],
        out_specs=pl.BlockSpec((tb, LANE), lambda b, h, t: (b * nt + t, h)),
        out_shape=jax.ShapeDtypeStruct((batch * seq, D_B), BF16),
        scratch_shapes=[pltpu.VMEM((HG_DIM, HG_DIM), F32), scr, scr, scr, scr, scr],
        compiler_params=_cparams(("parallel", "parallel", "arbitrary")),
        name="hgrn2",
    )(proj, proj, proj, proj, gamma, gn)


def _gelu_tanh(x):
    return 0.5 * x * (1.0 + jnp.tanh(math.sqrt(2.0 / math.pi) * (x + 0.044715 * (x * x * x))))


def _compress_kernel(x_ref, pe_ref, w1_ref, w2_ref, o_ref, *, nblk):
    half = CMP_LEN // 2
    top = jnp.zeros((nblk, CMP_HIDDEN), F32)
    bot = jnp.zeros((nblk, CMP_HIDDEN), F32)
    for r in range(half):
        xr = x_ref[pl.ds(r, nblk, stride=CMP_STRIDE), :]
        top += jnp.dot((xr + pe_ref[r:r + 1, :]).astype(BF16), w1_ref[r * NSA_DIM:(r + 1) * NSA_DIM, :],
                       preferred_element_type=F32)
        bot += jnp.dot((xr + pe_ref[half + r:half + r + 1, :]).astype(BF16),
                       w1_ref[(half + r) * NSA_DIM:(half + r + 1) * NSA_DIM, :], preferred_element_type=F32)
    pre = top + pltpu.roll(bot, nblk - 1, 0)
    o_ref[...] = jnp.dot(_gelu_tanh(pre).astype(BF16), w2_ref[...], preferred_element_type=F32)


def _compress(proj, pe, w1, w2, batch, seq):
    nblk = seq // CMP_STRIDE
    g = NSA_KV_HEADS
    return pl.pallas_call(
        functools.partial(_compress_kernel, nblk=nblk),
        grid=(batch, 2, g),
        in_specs=[pl.BlockSpec((seq, LANE), lambda b, s, j: (b, CB_KC + 2 * s + j)),
                  pl.BlockSpec((None, CMP_LEN, NSA_DIM), lambda b, s, j: (s, 0, 0)),
                  pl.BlockSpec((None, CMP_LEN * NSA_DIM, CMP_HIDDEN), lambda b, s, j: (s, 0, 0)),
                  pl.BlockSpec((None, CMP_HIDDEN, NSA_DIM), lambda b, s, j: (s, 0, 0))],
        out_specs=pl.BlockSpec((None, None, None, nblk, NSA_DIM), lambda b, s, j: (b, s, j, 0, 0)),
        out_shape=jax.ShapeDtypeStruct((batch, 2, g, nblk, NSA_DIM), F32),
        compiler_params=_cparams(("parallel", "parallel", "parallel")),
        name="nsa_compress",
    )(proj, pe, w1, w2)


def _nsa_kernel(q_ref, kc_ref, vc_ref, ks_ref, vs_ref, kw_ref, vw_ref, gl_ref, o_ref, selx_scr,
                *, tq, seq):
    grp = pl.program_id(1)
    qi = pl.program_id(2)
    nj = NSA_GROUP
    n_sel = seq // SLC_BLOCK
    ncmp_pad = seq // CMP_STRIDE
    scale = NSA_DIM ** -0.5
    qall = jnp.concatenate([q_ref[:, j * NSA_DIM:(j + 1) * NSA_DIM] for j in range(nj)], axis=0)
    qall = (qall * scale).astype(BF16)
    tpos = qi * tq + lax.broadcasted_iota(jnp.int32, (tq, 1), 0)

    def rep(x):
        return jnp.concatenate([x] * nj, axis=0)

    s = lax.dot_general(qall, kc_ref[...].astype(BF16), _NT, preferred_element_type=F32)
    n_id = lax.broadcasted_iota(jnp.int32, (tq, ncmp_pad), 1)
    cvis = rep(n_id * CMP_STRIDE + (CMP_LEN - 1) <= tpos)
    m = jnp.max(jnp.where(cvis, s, NEG), axis=-1, keepdims=True)
    p = jnp.exp(jnp.where(cvis, s - m, NEG))
    p = p / jnp.maximum(jnp.sum(p, axis=-1, keepdims=True), 1e-30)
    o_cmp = jnp.dot(p.astype(BF16), vc_ref[...].astype(BF16), preferred_element_type=F32)
    psum = p[0:tq] + p[tq:2 * tq] + p[2 * tq:3 * tq]
    cn = lax.broadcasted_iota(jnp.int32, (n_sel, ncmp_pad), 1) * CMP_STRIDE
    sb = lax.broadcasted_iota(jnp.int32, (n_sel, ncmp_pad), 0) * SLC_BLOCK
    overlap_t = jnp.where((cn < sb + SLC_BLOCK) & (cn + CMP_LEN > sb), 1.0, 0.0).astype(BF16)
    hi = psum.astype(BF16)
    r1 = psum - hi.astype(F32)
    mid = r1.astype(BF16)
    lo = (r1 - mid.astype(F32)).astype(BF16)
    imp = (lax.dot_general(hi, overlap_t, _NT, preferred_element_type=F32)
           + lax.dot_general(mid, overlap_t, _NT, preferred_element_type=F32)
           + lax.dot_general(lo, overlap_t, _NT, preferred_element_type=F32))

    blk = lax.broadcasted_iota(jnp.int32, (tq, n_sel), 1)
    cur = tpos // SLC_BLOCK
    forced = (blk == 0) | (blk == cur) | (blk == cur - 1)
    valid = blk * SLC_BLOCK <= tpos
    score = jnp.where(valid, imp + jnp.where(forced, FORCE_BONUS, 0.0), -jnp.inf)
    rank = jnp.zeros((tq, n_sel), F32)
    for i in range(n_sel):
        ci = score[:, i:i + 1]
        rank += jnp.where(blk > i, jnp.where(ci >= score, 1.0, 0.0), jnp.where(ci > score, 1.0, 0.0))
    sel = jnp.where(rank < float(min(SLC_TOPK, n_sel)), 1.0, 0.0).astype(BF16)
    eb = lax.broadcasted_iota(jnp.int32, (n_sel, seq), 0)
    ek = lax.broadcasted_iota(jnp.int32, (n_sel, seq), 1) // SLC_BLOCK
    expand = jnp.where(eb == ek, 1.0, 0.0).astype(BF16)
    selx_scr[...] = jnp.dot(sel, expand, preferred_element_type=F32)

    kiota = lax.broadcasted_iota(jnp.int32, (tq, tq), 1)

    def sel_body(kb, carry):
        k0 = pl.multiple_of(kb * tq, tq)
        keep = (selx_scr[:, pl.ds(k0, tq)] > 0.5) & (k0 + kiota <= tpos)
        bias = rep(jnp.where(keep, 0.0, NEG))
        return _flash_step(qall, ks_ref[pl.ds(k0, tq), :].astype(BF16), vs_ref[pl.ds(k0, tq), :].astype(BF16),
                           bias, carry)

    o_slc = _flash_out(lax.fori_loop(0, qi + 1, sel_body, _flash_init(nj * tq, NSA_DIM)))

    def win_body(kb, carry):
        k0 = pl.multiple_of(kb * tq, tq)
        dist = tpos - (k0 + kiota)
        bias = rep(jnp.where((dist >= 0) & (dist < WINDOW), 0.0, NEG))
        return _flash_step(qall, kw_ref[pl.ds(k0, tq), :].astype(BF16), vw_ref[pl.ds(k0, tq), :].astype(BF16),
                           bias, carry)

    m_w, l_w, acc_w = lax.fori_loop(jnp.maximum(qi - WINDOW // tq, 0), qi + 1, win_body,
                                    _flash_init(nj * tq, NSA_DIM))
    n_pad = rep(jnp.maximum(WINDOW - 1 - tpos, 0)).astype(F32)
    m_p = jnp.where(n_pad > 0.0, jnp.maximum(m_w, 0.0), m_w)
    a_p = jnp.exp(m_w - m_p)
    l_p = jnp.where(n_pad > 0.0, n_pad * jnp.exp(-jnp.maximum(m_p, 0.0)), 0.0)
    o_win = (acc_w * a_p) / jnp.maximum(l_w * a_p + l_p, 1e-30)

    gate = jax.nn.sigmoid(gl_ref[...])
    for j in range(nj):
        def gcol(br):
            c0 = 3 * j + br
            c1 = 3 * (nj + j) + br
            return jnp.where(grp == 0, gate[:, c0:c0 + 1], gate[:, c1:c1 + 1])
        rows = slice(j * tq, (j + 1) * tq)
        y = gcol(0) * o_cmp[rows] + gcol(1) * o_slc[rows] + gcol(2) * o_win[rows]
        o_ref[:, j * NSA_DIM:(j + 1) * NSA_DIM] = y.astype(BF16)


def _nsa(proj, cmp, batch, seq, tq=128):
    nq = seq // tq
    g = NSA_KV_HEADS
    gw = NSA_GROUP * NSA_DIM
    ncmp_pad = seq // CMP_STRIDE

    def kv(cb):
        return pl.BlockSpec((seq, LANE), lambda b, j, i: (b, cb + j))

    def cmp_spec(which):
        return pl.BlockSpec((None, None, None, ncmp_pad, NSA_DIM), lambda b, j, i: (b, which, j, 0, 0))

    return pl.pallas_call(
        functools.partial(_nsa_kernel, tq=tq, seq=seq),
        grid=(batch, g, nq),
        in_specs=[pl.BlockSpec((tq, gw), lambda b, j, i: (b * nq + i, CB_CQ * LANE // gw + j)),
                  cmp_spec(0), cmp_spec(1),
                  kv(CB_KS), kv(CB_VS), kv(CB_KW), kv(CB_VW),
                  pl.BlockSpec((tq, LANE), lambda b, j, i: (b * nq + i, CB_CG))],
        out_specs=pl.BlockSpec((tq, gw), lambda b, j, i: (b * nq + i, j)),
        out_shape=jax.ShapeDtypeStruct((batch * seq, D_C), BF16),
        scratch_shapes=[pltpu.VMEM((tq, seq), F32)],
        compiler_params=_cparams(("parallel", "parallel", "arbitrary")),
        name="nsa",
    )(proj, cmp, cmp, proj, proj, proj, proj, proj)


def kernel(x, attn_norm, w_in, da_lam_q1, da_lam_k1, da_lam_q2, da_lam_k2, da_norm, hg_gamma, hg_norm,
           nsa_pe_k, nsa_pe_v, nsa_ck_w1, nsa_ck_w2, nsa_cv_w1, nsa_cv_w2, w_out, ffn_norm,
           w_gate, w_up, w_down, final_norm):
    batch, seq, d = x.shape
    depth = w_in.shape[0]
    assert (CB_CQ * LANE) % (NSA_GROUP * NSA_DIM) == 0 and CB_CG * LANE + NSA_HEADS * 3 == D_IN
    xf = x.reshape(batch * seq, d)
    for l in range(depth):
        w_in_l = jnp.pad(w_in[l], ((0, 0), (0, D_IN_PAD - D_IN))).astype(BF16)
        proj = _norm_matmul(xf, attn_norm[l], w_in_l)
        ya = _diff_attn(proj, da_lam_q1[l], da_lam_k1[l], da_lam_q2[l], da_lam_k2[l], da_norm[l],
                        l, batch, seq)
        yb = _hgrn(proj, hg_gamma, hg_norm[l], l, batch, seq)
        cmp = _compress(proj, jnp.stack([nsa_pe_k[l], nsa_pe_v[l]]),
                        jnp.stack([nsa_ck_w1[l], nsa_cv_w1[l]]).astype(BF16),
                        jnp.stack([nsa_ck_w2[l], nsa_cv_w2[l]]).astype(BF16), batch, seq)
        yc = _nsa(proj, cmp, batch, seq)
        xf = _out_proj(ya, yb, yc, w_out[l].astype(BF16), xf)
        u = _ffn_up(xf, ffn_norm[l], w_gate[l].astype(BF16), w_up[l].astype(BF16))
        xf = _matmul_res(u, w_down[l].astype(BF16), xf)
    return _final_norm(xf, final_norm).reshape(batch, seq, d)
```

```python
import functools
import math

import jax
import jax.numpy as jnp
import numpy as np
from jax import lax
from jax.experimental import pallas as pl
from jax.experimental.pallas import tpu as pltpu

F32 = jnp.float32
BF16 = jnp.bfloat16

D_MODEL = 2048
DA_HEADS = 4
DA_QK_DIM = 64
DA_V_DIM = 128
HG_HEADS = 6
HG_DIM = 128
NSA_HEADS = 6
NSA_KV_HEADS = 2
NSA_GROUP = NSA_HEADS // NSA_KV_HEADS
NSA_DIM = 128
CMP_LEN = 32
CMP_STRIDE = 16
CMP_HIDDEN = 256
SLC_BLOCK = 64
SLC_TOPK = 16
WINDOW = 512
FORCE_BONUS = 1.0e4
D_A = DA_HEADS * DA_V_DIM
D_B = HG_HEADS * HG_DIM
D_C = NSA_HEADS * NSA_DIM
D_FF = ((8 * D_MODEL // 3 + 255) // 256) * 256
D_IN = 3 * D_A + 4 * D_B + D_C + 6 * NSA_KV_HEADS * NSA_DIM + NSA_HEADS * 3

LANE = 128
D_IN_PAD = 7168
CB_AQ, CB_AK, CB_AV = 0, 4, 8
CB_BF, CB_BQ, CB_BI, CB_BG = 12, 18, 24, 30
CB_CQ = 36
CB_KC, CB_VC, CB_KS, CB_VS, CB_KW, CB_VW = 42, 44, 46, 48, 50, 52
CB_CG = 54

EPS = 1e-6
NEG = -1e30
VMEM_LIMIT = 56 * 1024 * 1024

_NT = (((1,), (1,)), ((), ()))


def _cparams(sem):
    return pltpu.CompilerParams(dimension_semantics=sem, vmem_limit_bytes=VMEM_LIMIT)


def _rms_rows(x, gain):
    return x * lax.rsqrt(jnp.mean(x * x, axis=-1, keepdims=True) + EPS) * gain


def _norm_matmul_kernel(x_ref, g_ref, w_ref, o_ref, h_scr):
    @pl.when(pl.program_id(1) == 0)
    def _():
        h_scr[...] = _rms_rows(x_ref[...], g_ref[...]).astype(BF16)

    o_ref[...] = jnp.dot(h_scr[...], w_ref[...], preferred_element_type=F32)


def _norm_matmul(x, gain, w, tm=1024, tn=512):
    m, k = x.shape
    n = w.shape[1]
    return pl.pallas_call(
        _norm_matmul_kernel,
        grid=(m // tm, n // tn),
        in_specs=[pl.BlockSpec((tm, k), lambda i, j: (i, 0)),
                  pl.BlockSpec((1, k), lambda i, j: (0, 0)),
                  pl.BlockSpec((k, tn), lambda i, j: (0, j))],
        out_specs=pl.BlockSpec((tm, tn), lambda i, j: (i, j)),
        out_shape=jax.ShapeDtypeStruct((m, n), F32),
        scratch_shapes=[pltpu.VMEM((tm, k), BF16)],
        compiler_params=_cparams(("parallel", "arbitrary")),
        name="norm_in_proj",
    )(x, gain.reshape(1, k), w)


def _ffn_up_kernel(x_ref, g_ref, wg_ref, wu_ref, o_ref, h_scr):
    @pl.when(pl.program_id(1) == 0)
    def _():
        h_scr[...] = _rms_rows(x_ref[...], g_ref[...]).astype(BF16)

    h = h_scr[...]
    a = jnp.dot(h, wg_ref[...], preferred_element_type=F32)
    b = jnp.dot(h, wu_ref[...], preferred_element_type=F32)
    o_ref[...] = (a * jax.nn.sigmoid(a) * b).astype(BF16)


def _ffn_up(x, gain, wg, wu, tm=1024, tn=512):
    m, k = x.shape
    n = wg.shape[1]
    return pl.pallas_call(
        _ffn_up_kernel,
        grid=(m // tm, n // tn),
        in_specs=[pl.BlockSpec((tm, k), lambda i, j: (i, 0)),
                  pl.BlockSpec((1, k), lambda i, j: (0, 0)),
                  pl.BlockSpec((k, tn), lambda i, j: (0, j)),
                  pl.BlockSpec((k, tn), lambda i, j: (0, j))],
        out_specs=pl.BlockSpec((tm, tn), lambda i, j: (i, j)),
        out_shape=jax.ShapeDtypeStruct((m, n), BF16),
        scratch_shapes=[pltpu.VMEM((tm, k), BF16)],
        compiler_params=_cparams(("parallel", "arbitrary")),
        name="ffn_up",
    )(x, gain.reshape(1, k), wg, wu)


def _matmul_res_kernel(a_ref, w_ref, r_ref, o_ref):
    o_ref[...] = r_ref[...] + jnp.dot(a_ref[...], w_ref[...], preferred_element_type=F32)


def _matmul_res(a, w, res, tm=512, tn=512):
    m, k = a.shape
    n = w.shape[1]
    return pl.pallas_call(
        _matmul_res_kernel,
        grid=(m // tm, n // tn),
        in_specs=[pl.BlockSpec((tm, k), lambda i, j: (i, 0)),
                  pl.BlockSpec((k, tn), lambda i, j: (0, j)),
                  pl.BlockSpec((tm, tn), lambda i, j: (i, j))],
        out_specs=pl.BlockSpec((tm, tn), lambda i, j: (i, j)),
        out_shape=jax.ShapeDtypeStruct((m, n), F32),
        compiler_params=_cparams(("parallel", "arbitrary")),
        name="ffn_down",
    )(a, w, res)


def _out_proj_kernel(ya_ref, yb_ref, yc_ref, w_ref, r_ref, o_ref):
    acc = jnp.dot(ya_ref[...], w_ref[0:D_A, :], preferred_element_type=F32)
    acc += jnp.dot(yb_ref[...], w_ref[D_A:D_A + D_B, :], preferred_element_type=F32)
    acc += jnp.dot(yc_ref[...], w_ref[D_A + D_B:, :], preferred_element_type=F32)
    o_ref[...] = r_ref[...] + acc


def _out_proj(ya, yb, yc, w, res, tm=1024, tn=512):
    m = ya.shape[0]
    k, n = w.shape
    return pl.pallas_call(
        _out_proj_kernel,
        grid=(m // tm, n // tn),
        in_specs=[pl.BlockSpec((tm, D_A), lambda i, j: (i, 0)),
                  pl.BlockSpec((tm, D_B), lambda i, j: (i, 0)),
                  pl.BlockSpec((tm, D_C), lambda i, j: (i, 0)),
                  pl.BlockSpec((k, tn), lambda i, j: (0, j)),
                  pl.BlockSpec((tm, tn), lambda i, j: (i, j))],
        out_specs=pl.BlockSpec((tm, tn), lambda i, j: (i, j)),
        out_shape=jax.ShapeDtypeStruct((m, n), F32),
        compiler_params=_cparams(("parallel", "arbitrary")),
        name="out_proj",
    )(ya, yb, yc, w, res)


def _final_norm_kernel(x_ref, g_ref, o_ref):
    o_ref[...] = _rms_rows(x_ref[...], g_ref[...])


def _final_norm(x, gain, tm=512):
    m, k = x.shape
    return pl.pallas_call(
        _final_norm_kernel,
        grid=(m // tm,),
        in_specs=[pl.BlockSpec((tm, k), lambda i: (i, 0)),
                  pl.BlockSpec((1, k), lambda i: (0, 0))],
        out_specs=pl.BlockSpec((tm, k), lambda i: (i, 0)),
        out_shape=jax.ShapeDtypeStruct((m, k), F32),
        compiler_params=_cparams(("parallel",)),
        name="final_norm",
    )(x, gain.reshape(1, k))


def _softmax_block(s, bias, m, l):
    if bias is not None:
        s = s + bias
    m_new = jnp.maximum(m, jnp.max(s, axis=0, keepdims=True))
    alpha = jnp.exp(m - m_new)
    p = jnp.exp(s - m_new)
    return p.astype(BF16), alpha, m_new, alpha * l + jnp.sum(p, axis=0, keepdims=True)


def _attend_range(qall, kb_scr, vt_scr, lo, hi, bias_fn, last_bias_fn, s_scr, p_scr, acc_scr, tk):
    rows = qall.shape[0]

    def scores(b):
        k0 = pl.multiple_of(b * tk, tk)
        return lax.dot_general(kb_scr[pl.ds(k0, tk), :], qall, _NT, preferred_element_type=F32)

    def values(b, p):
        k0 = pl.multiple_of(b * tk, tk)
        return jnp.dot(vt_scr[:, pl.ds(k0, tk)], p, preferred_element_type=F32)

    acc_scr[...] = jnp.zeros_like(acc_scr)
    s_scr[lo % 2] = scores(lo)
    p_scr[(lo + 1) % 2] = jnp.zeros(p_scr.shape[1:], BF16)

    def body(b, carry):
        m, l, alpha_prev = carry
        cur = b % 2
        pv_prev = values(jnp.maximum(b - 1, lo), p_scr[1 - cur])
        p, alpha, m, l = _softmax_block(s_scr[cur], bias_fn(b), m, l)
        acc_scr[...] = alpha_prev * acc_scr[...] + pv_prev
        p_scr[cur] = p
        s_scr[1 - cur] = scores(b + 1)
        return m, l, alpha

    init = (jnp.full((1, rows), NEG, F32), jnp.zeros((1, rows), F32), jnp.ones((1, rows), F32))
    m, l, alpha_prev = lax.fori_loop(lo, hi - 1, body, init)
    b = hi - 1
    cur = b % 2
    pv_prev = values(jnp.maximum(b - 1, lo), p_scr[1 - cur])
    p, alpha, m, l = _softmax_block(s_scr[cur], last_bias_fn(b), m, l)
    acc_scr[...] = alpha * (alpha_prev * acc_scr[...] + pv_prev) + values(b, p)
    return m, l


def _stage_kv(k_ref, v_ref, kb_scr, vt_scr, seq, tk):
    for c in range(seq // tk):
        rows = slice(c * tk, (c + 1) * tk)
        kb_scr[rows, :] = k_ref[rows, :].astype(BF16)
        vt_scr[:, rows] = v_ref[rows, :].T.astype(BF16)


def _diff_attn_kernel(q_ref, k_ref, v_ref, lq1_ref, lk1_ref, lq2_ref, lk2_ref, gn_ref, o_ref,
                      kb_scr, vt_scr, s_scr, p_scr, acc_scr, *, lam_init, tq, seq):
    h = pl.program_id(1)
    i = pl.program_id(2)
    tk = tq

    @pl.when(i == 0)
    def _():
        _stage_kv(k_ref, v_ref, kb_scr, vt_scr, seq, tk)

    q = q_ref[...] * (DA_QK_DIM ** -0.5)
    lane = lax.broadcasted_iota(jnp.int32, q.shape, 1)
    qbd = jnp.concatenate([jnp.where(lane < DA_QK_DIM, q, 0.0), jnp.where(lane >= DA_QK_DIM, q, 0.0)],
                          axis=0).astype(BF16)
    tpos = i * tq + lax.broadcasted_iota(jnp.int32, (1, tq), 1)
    kio = lax.broadcasted_iota(jnp.int32, (tk, 1), 0)

    def causal(b):
        keep = jnp.where(b * tk + kio <= tpos, 0.0, NEG)
        return jnp.concatenate([keep, keep], axis=1)

    m, l = _attend_range(qbd, kb_scr, vt_scr, 0, i + 1, lambda b: None, causal, s_scr, p_scr, acc_scr, tk)
    o_t = acc_scr[...] / jnp.maximum(l, 1e-30)
    lam = (jnp.exp(jnp.sum(lq1_ref[...] * lk1_ref[...], axis=-1, keepdims=True))
           - jnp.exp(jnp.sum(lq2_ref[...] * lk2_ref[...], axis=-1, keepdims=True)) + lam_init)
    o = (o_t[:, :tq] - lam * o_t[:, tq:]).T
    y = _rms_rows(o, gn_ref[pl.ds(h, 1), :]) * (1.0 - lam_init)
    o_ref[...] = y.astype(BF16)


def _diff_attn(proj, lq1, lk1, lq2, lk2, gn, layer, batch, seq, tq=256):
    nq = seq // tq
    lam_init = 0.8 - 0.6 * math.exp(-0.3 * layer)
    vec = pl.BlockSpec((1, DA_QK_DIM), lambda b, h, i: (0, 0))
    return pl.pallas_call(
        functools.partial(_diff_attn_kernel, lam_init=lam_init, tq=tq, seq=seq),
        grid=(batch, DA_HEADS, nq),
        in_specs=[pl.BlockSpec((tq, LANE), lambda b, h, i: (b * nq + i, CB_AQ + h)),
                  pl.BlockSpec((seq, LANE), lambda b, h, i: (b, CB_AK + h)),
                  pl.BlockSpec((seq, LANE), lambda b, h, i: (b, CB_AV + h)),
                  vec, vec, vec, vec,
                  pl.BlockSpec((DA_HEADS, DA_V_DIM), lambda b, h, i: (0, 0))],
        out_specs=pl.BlockSpec((tq, LANE), lambda b, h, i: (b * nq + i, h)),
        out_shape=jax.ShapeDtypeStruct((batch * seq, D_A), BF16),
        scratch_shapes=[pltpu.VMEM((seq, LANE), BF16), pltpu.VMEM((DA_V_DIM, seq), BF16),
                        pltpu.VMEM((2, tq, 2 * tq), F32), pltpu.VMEM((2, tq, 2 * tq), BF16),
                        pltpu.VMEM((DA_V_DIM, 2 * tq), F32)],
        compiler_params=_cparams(("parallel", "parallel", "arbitrary")),
        name="diff_attn",
    )(proj, proj, proj, lq1.reshape(1, -1), lk1.reshape(1, -1), lq2.reshape(1, -1), lk2.reshape(1, -1), gn)


HG_LEVELS = (64, 32, 16, 8, 4, 2, 1)
HG_SUB = 2 * HG_LEVELS[0]


def _hgrn_consts():
    tb = HG_SUB
    t = np.arange(tb)[:, None]
    u = np.arange(tb)[None, :]
    sums, owns = [], []
    for c in HG_LEVELS:
        mid = (t // (2 * c)) * (2 * c) + c
        second = (t % (2 * c)) >= c
        sums.append(np.where(second, (u >= mid) & (u <= t), (u > t) & (u < mid)))
        owns.append(((t // (2 * c)) == (u // (2 * c))) & second & ((u % (2 * c)) < c))
    sums.append(u <= t)
    return (jnp.asarray(np.concatenate(sums, 0).astype(np.float32), BF16),
            jnp.asarray(np.stack(owns).astype(np.float32)))


def _hgrn_kernel(f_ref, q_ref, i_ref, g_ref, gam_ref, gn_ref, sums_ref, owns_ref, o_ref, st_scr,
                 *, layer, ts):
    h = pl.program_id(1)
    tb = HG_SUB
    nlev = len(HG_LEVELS)

    @pl.when(pl.program_id(2) == 0)
    def _():
        st_scr[...] = jnp.zeros_like(st_scr)

    if layer > 0:
        gam = gam_ref[...]
        e = jnp.exp(gam - jnp.max(gam, axis=0, keepdims=True))
        lb = jnp.sum(e[1:layer + 1], axis=0, keepdims=True) / jnp.sum(e, axis=0, keepdims=True)
        log_lb = jnp.log(lb)
        log_1mlb = jnp.log(1.0 - lb)
    gain = gn_ref[pl.ds(h, 1), :]
    st = st_scr[...]

    for sb in range(ts // tb):
        rows = slice(sb * tb, (sb + 1) * tb)
        z = f_ref[rows, :]
        soft = jnp.log(1.0 + jnp.exp(-jnp.abs(z)))
        logsig = jnp.minimum(z, 0.0) - soft
        logsig_neg = jnp.minimum(-z, 0.0) - soft
        if layer == 0:
            logf = logsig
            key = jnp.exp(logsig_neg)
        else:
            b = log_1mlb + logsig
            logf = jnp.maximum(log_lb, b) + jnp.log(1.0 + jnp.exp(-jnp.abs(log_lb - b)))
            key = (1.0 - lb) * jnp.exp(logsig_neg)
        hi = logf.astype(BF16)
        r1 = logf - hi.astype(F32)
        mid = r1.astype(BF16)
        lo = (r1 - mid.astype(F32)).astype(BF16)
        parts = jnp.dot(sums_ref[...], jnp.concatenate([hi, mid, lo], axis=1), preferred_element_type=F32)

        def expo(i):
            blk = parts[i * tb:(i + 1) * tb]
            return blk[:, 0:HG_DIM] + blk[:, HG_DIM:2 * HG_DIM] + blk[:, 2 * HG_DIM:3 * HG_DIM]

        q = q_ref[rows, :]
        v = i_ref[rows, :]
        vb = v.astype(BF16)
        scores = None
        for i in range(nlev):
            w = jnp.exp(expo(i))
            a = lax.dot_general((q * w).astype(BF16), (key * w).astype(BF16), _NT, preferred_element_type=F32)
            a = a * owns_ref[i]
            scores = a if scores is None else scores + a
        cum = expo(nlev)
        o = jnp.dot(scores.astype(BF16), vb, preferred_element_type=F32)
        o = o + jnp.sum(q * key, axis=-1, keepdims=True) * v
        o = o + lax.dot_general((q * jnp.exp(cum)).astype(BF16), st.astype(BF16), _NT, preferred_element_type=F32)
        gt = g_ref[rows, :]
        o_ref[rows, :] = (_rms_rows(o, gain) * (gt * jax.nn.sigmoid(gt))).astype(BF16)
        last = cum[tb - 1:tb, :]
        ke = (key * jnp.exp(last - cum)).astype(BF16)
        st = st * jnp.exp(last) + jnp.dot(v.T.astype(BF16), ke, preferred_element_type=F32)

    st_scr[...] = st


def _hgrn(proj, gamma, gn, layer, batch, seq, ts=512):
    nt = seq // ts
    depth = gamma.shape[0]
    sums, owns = _hgrn_consts()

    def blk(cb):
        return pl.BlockSpec((ts, LANE), lambda b, h, t: (b * nt + t, cb + h))

    return pl.pallas_call(
        functools.partial(_hgrn_kernel, layer=layer, ts=ts),
        grid=(batch, HG_HEADS, nt),
        in_specs=[blk(CB_BF), blk(CB_BQ), blk(CB_BI), blk(CB_BG),
                  pl.BlockSpec((depth, LANE), lambda b, h, t: (0, h)),
                  pl.BlockSpec((HG_HEADS, HG_DIM), lambda b, h, t: (0, 0)),
                  pl.BlockSpec(sums.shape, lambda b, h, t: (0, 0)),
                  pl.BlockSpec(owns.shape, lambda b, h, t: (0, 0, 0))],
        out_specs=pl.BlockSpec((ts, LANE), lambda b, h, t: (b * nt + t, h)),
        out_shape=jax.ShapeDtypeStruct((batch * seq, D_B), BF16),
        scratch_shapes=[pltpu.VMEM((HG_DIM, HG_DIM), F32)],
        compiler_params=_cparams(("parallel", "parallel", "arbitrary")),
        name="hgrn2",
    )(proj, proj, proj, proj, gamma, gn, sums, owns)


def _gelu_tanh(x):
    return 0.5 * x * (1.0 + jnp.tanh(math.sqrt(2.0 / math.pi) * (x + 0.044715 * (x * x * x))))


def _compress_kernel(x_ref, pe_ref, w1_ref, w2_ref, o_ref, *, nblk):
    half = CMP_LEN // 2
    top = jnp.zeros((nblk, CMP_HIDDEN), F32)
    bot = jnp.zeros((nblk, CMP_HIDDEN), F32)
    for r in range(half):
        xr = x_ref[pl.ds(r, nblk, stride=CMP_STRIDE), :]
        top += jnp.dot((xr + pe_ref[r:r + 1, :]).astype(BF16), w1_ref[r * NSA_DIM:(r + 1) * NSA_DIM, :],
                       preferred_element_type=F32)
        bot += jnp.dot((xr + pe_ref[half + r:half + r + 1, :]).astype(BF16),
                       w1_ref[(half + r) * NSA_DIM:(half + r + 1) * NSA_DIM, :], preferred_element_type=F32)
    pre = top + pltpu.roll(bot, nblk - 1, 0)
    o_ref[...] = jnp.dot(_gelu_tanh(pre).astype(BF16), w2_ref[...], preferred_element_type=F32)


def _compress(proj, pe, w1, w2, batch, seq):
    nblk = seq // CMP_STRIDE
    g = NSA_KV_HEADS
    return pl.pallas_call(
        functools.partial(_compress_kernel, nblk=nblk),
        grid=(batch, 2, g),
        in_specs=[pl.BlockSpec((seq, LANE), lambda b, s, j: (b, CB_KC + 2 * s + j)),
                  pl.BlockSpec((None, CMP_LEN, NSA_DIM), lambda b, s, j: (s, 0, 0)),
                  pl.BlockSpec((None, CMP_LEN * NSA_DIM, CMP_HIDDEN), lambda b, s, j: (s, 0, 0)),
                  pl.BlockSpec((None, CMP_HIDDEN, NSA_DIM), lambda b, s, j: (s, 0, 0))],
        out_specs=pl.BlockSpec((None, None, None, nblk, NSA_DIM), lambda b, s, j: (b, s, j, 0, 0)),
        out_shape=jax.ShapeDtypeStruct((batch, 2, g, nblk, NSA_DIM), F32),
        compiler_params=_cparams(("parallel", "parallel", "parallel")),
        name="nsa_compress",
    )(proj, pe, w1, w2)


def _nsa_kernel(q_ref, kc_ref, vc_ref, ks_ref, vs_ref, kw_ref, vw_ref, gl_ref, o_ref,
                ksb_scr, vst_scr, kwb_scr, vwt_scr, selb_scr, s_scr, p_scr, sw_scr, pw_scr, accs_scr, accw_scr, *, tq, seq):
    grp = pl.program_id(1)
    qi = pl.program_id(2)
    nj = NSA_GROUP
    tk = tq
    n_sel = seq // SLC_BLOCK
    ncmp_pad = seq // CMP_STRIDE

    @pl.when(qi == 0)
    def _():
        _stage_kv(ks_ref, vs_ref, ksb_scr, vst_scr, seq, tk)
        _stage_kv(kw_ref, vw_ref, kwb_scr, vwt_scr, seq, tk)

    qall = jnp.concatenate([q_ref[:, j * NSA_DIM:(j + 1) * NSA_DIM] for j in range(nj)], axis=0)
    qall = (qall * (NSA_DIM ** -0.5)).astype(BF16)
    tpos = qi * tq + lax.broadcasted_iota(jnp.int32, (1, tq), 1)

    def rep(x):
        return jnp.concatenate([x] * nj, axis=1)

    sc = lax.dot_general(kc_ref[...].astype(BF16), qall, _NT, preferred_element_type=F32)
    n_end = lax.broadcasted_iota(jnp.int32, (ncmp_pad, 1), 0) * CMP_STRIDE + (CMP_LEN - 1)
    cvis = rep(n_end <= tpos)
    m = jnp.max(jnp.where(cvis, sc, NEG), axis=0, keepdims=True)
    p = jnp.exp(jnp.where(cvis, sc - m, NEG))
    p = p / jnp.maximum(jnp.sum(p, axis=0, keepdims=True), 1e-30)
    o_cmp = jnp.dot(vc_ref[...].T.astype(BF16), p.astype(BF16), preferred_element_type=F32)
    psum = p[:, 0:tq] + p[:, tq:2 * tq] + p[:, 2 * tq:3 * tq]
    cn = lax.broadcasted_iota(jnp.int32, (n_sel, ncmp_pad), 1) * CMP_STRIDE
    sb = lax.broadcasted_iota(jnp.int32, (n_sel, ncmp_pad), 0) * SLC_BLOCK
    overlap_t = jnp.where((cn < sb + SLC_BLOCK) & (cn + CMP_LEN > sb), 1.0, 0.0).astype(BF16)
    hi = psum.astype(BF16)
    r1 = psum - hi.astype(F32)
    mid = r1.astype(BF16)
    lo = (r1 - mid.astype(F32)).astype(BF16)
    imp = (jnp.dot(overlap_t, hi, preferred_element_type=F32)
           + jnp.dot(overlap_t, mid, preferred_element_type=F32)
           + jnp.dot(overlap_t, lo, preferred_element_type=F32))

    blk = lax.broadcasted_iota(jnp.int32, (n_sel, 1), 0)
    cur = tpos // SLC_BLOCK
    forced = (blk == 0) | (blk == cur) | (blk == cur - 1)
    valid = blk * SLC_BLOCK <= tpos
    score = jnp.where(valid, imp + jnp.where(forced, FORCE_BONUS, 0.0), -jnp.inf)
    rank = jnp.zeros((n_sel, tq), F32)
    for i in range(n_sel):
        ci = score[i:i + 1, :]
        rank += jnp.where(blk > i, jnp.where(ci >= score, 1.0, 0.0), jnp.where(ci > score, 1.0, 0.0))
    sel_mask = jnp.where(rank < float(min(SLC_TOPK, n_sel)), 0.0, NEG)
    for r in range(n_sel):
        selb_scr[r * SLC_BLOCK:(r + 1) * SLC_BLOCK, :] = jnp.broadcast_to(sel_mask[r:r + 1, :], (SLC_BLOCK, tq))

    kio = lax.broadcasted_iota(jnp.int32, (tk, 1), 0)

    def sel_bias(kb):
        return selb_scr[pl.ds(pl.multiple_of(kb * tk, tk), tk), :]

    m_s, l_s = _attend_range(qall, ksb_scr, vst_scr, 0, qi + 1, lambda b: rep(sel_bias(b)),
                             lambda b: rep(jnp.where(b * tk + kio <= tpos, sel_bias(b), NEG)),
                             s_scr, p_scr, accs_scr, tk)
    o_slc = accs_scr[...] / jnp.maximum(l_s, 1e-30)

    def win_bias(b):
        dist = tpos - (b * tk + kio)
        return rep(jnp.where((dist >= 0) & (dist < WINDOW), 0.0, NEG))

    m_w, l_w = _attend_range(qall, kwb_scr, vwt_scr, jnp.maximum(qi - WINDOW // tk, 0), qi + 1,
                             win_bias, win_bias, sw_scr, pw_scr, accw_scr, tk)
    n_pad = rep(jnp.maximum(WINDOW - 1 - tpos, 0)).astype(F32)
    m_p = jnp.where(n_pad > 0.0, jnp.maximum(m_w, 0.0), m_w)
    a_p = jnp.exp(m_w - m_p)
    l_p = jnp.where(n_pad > 0.0, n_pad * jnp.exp(-jnp.maximum(m_p, 0.0)), 0.0)
    o_win = (accw_scr[...] * a_p) / jnp.maximum(l_w * a_p + l_p, 1e-30)

    gate = jax.nn.sigmoid(gl_ref[...].T)
    for j in range(nj):
        def grow(br):
            c0 = 3 * j + br
            c1 = 3 * (nj + j) + br
            return jnp.where(grp == 0, gate[c0:c0 + 1, :], gate[c1:c1 + 1, :])
        lanes = slice(j * tq, (j + 1) * tq)
        y_t = grow(0) * o_cmp[:, lanes] + grow(1) * o_slc[:, lanes] + grow(2) * o_win[:, lanes]
        o_ref[:, j * NSA_DIM:(j + 1) * NSA_DIM] = y_t.T.astype(BF16)


def _nsa(proj, cmp, batch, seq, tq=256):
    nq = seq // tq
    g = NSA_KV_HEADS
    gw = NSA_GROUP * NSA_DIM
    ncmp_pad = seq // CMP_STRIDE

    def kv(cb):
        return pl.BlockSpec((seq, LANE), lambda b, j, i: (b, cb + j))

    def cmp_spec(which):
        return pl.BlockSpec((None, None, None, ncmp_pad, NSA_DIM), lambda b, j, i: (b, which, j, 0, 0))

    kb_scr = pltpu.VMEM((seq, LANE), BF16)
    vt_scr = pltpu.VMEM((NSA_DIM, seq), BF16)
    acc_scr = pltpu.VMEM((NSA_DIM, NSA_GROUP * tq), F32)
    s_scr = pltpu.VMEM((2, tq, NSA_GROUP * tq), F32)
    p_scr = pltpu.VMEM((2, tq, NSA_GROUP * tq), BF16)
    return pl.pallas_call(
        functools.partial(_nsa_kernel, tq=tq, seq=seq),
        grid=(batch, g, nq),
        in_specs=[pl.BlockSpec((tq, gw), lambda b, j, i: (b * nq + i, CB_CQ * LANE // gw + j)),
                  cmp_spec(0), cmp_spec(1),
                  kv(CB_KS), kv(CB_VS), kv(CB_KW), kv(CB_VW),
                  pl.BlockSpec((tq, LANE), lambda b, j, i: (b * nq + i, CB_CG))],
        out_specs=pl.BlockSpec((tq, gw), lambda b, j, i: (b * nq + i, j)),
        out_shape=jax.ShapeDtypeStruct((batch * seq, D_C), BF16),
        scratch_shapes=[kb_scr, vt_scr, kb_scr, vt_scr, pltpu.VMEM((seq, tq), F32),
                        s_scr, p_scr, s_scr, p_scr, acc_scr, acc_scr],
        compiler_params=_cparams(("parallel", "parallel", "arbitrary")),
        name="nsa",
    )(proj, cmp, cmp, proj, proj, proj, proj, proj)


def kernel(x, attn_norm, w_in, da_lam_q1, da_lam_k1, da_lam_q2, da_lam_k2, da_norm, hg_gamma, hg_norm,
           nsa_pe_k, nsa_pe_v, nsa_ck_w1, nsa_ck_w2, nsa_cv_w1, nsa_cv_w2, w_out, ffn_norm,
           w_gate, w_up, w_down, final_norm):
    batch, seq, d = x.shape
    depth = w_in.shape[0]
    assert (CB_CQ * LANE) % (NSA_GROUP * NSA_DIM) == 0 and CB_CG * LANE + NSA_HEADS * 3 == D_IN
    xf = x.reshape(batch * seq, d)
    for l in range(depth):
        w_in_l = jnp.pad(w_in[l], ((0, 0), (0, D_IN_PAD - D_IN))).astype(BF16)
        proj = _norm_matmul(xf, attn_norm[l], w_in_l)
        ya = _diff_attn(proj, da_lam_q1[l], da_lam_k1[l], da_lam_q2[l], da_lam_k2[l], da_norm[l],
                        l, batch, seq)
        yb = _hgrn(proj, hg_gamma, hg_norm[l], l, batch, seq)
        cmp = _compress(proj, jnp.stack([nsa_pe_k[l], nsa_pe_v[l]]),
                        jnp.stack([nsa_ck_w1[l], nsa_cv_w1[l]]).astype(BF16),
                        jnp.stack([nsa_ck_w2[l], nsa_cv_w2[l]]).astype(BF16), batch, seq)
        yc = _nsa(proj, cmp, batch, seq)
        xf = _out_proj(ya, yb, yc, w_out[l].astype(BF16), xf)
        u = _ffn_up(xf, ffn_norm[l], w_gate[l].astype(BF16), w_up[l].astype(BF16))
        xf = _matmul_res(u, w_down[l].astype(BF16), xf)
    return _final_norm(xf, final_norm).reshape(batch, seq, d)
```

```python
import functools
import math

import jax
import jax.numpy as jnp
import numpy as np
from jax import lax
from jax.experimental import pallas as pl
from jax.experimental.pallas import tpu as pltpu

F32 = jnp.float32
BF16 = jnp.bfloat16

D_MODEL = 2048
DA_HEADS = 4
DA_QK_DIM = 64
DA_V_DIM = 128
HG_HEADS = 6
HG_DIM = 128
NSA_HEADS = 6
NSA_KV_HEADS = 2
NSA_GROUP = NSA_HEADS // NSA_KV_HEADS
NSA_DIM = 128
CMP_LEN = 32
CMP_STRIDE = 16
CMP_HIDDEN = 256
SLC_BLOCK = 64
SLC_TOPK = 16
WINDOW = 512
FORCE_BONUS = 1.0e4
D_A = DA_HEADS * DA_V_DIM
D_B = HG_HEADS * HG_DIM
D_C = NSA_HEADS * NSA_DIM
D_FF = ((8 * D_MODEL // 3 + 255) // 256) * 256
D_IN = 3 * D_A + 4 * D_B + D_C + 6 * NSA_KV_HEADS * NSA_DIM + NSA_HEADS * 3

LANE = 128
D_IN_PAD = 7168
CB_AQ, CB_AK, CB_AV = 0, 4, 8
CB_BF, CB_BQ, CB_BI, CB_BG = 12, 18, 24, 30
CB_CQ = 36
CB_KC, CB_VC, CB_KS, CB_VS, CB_KW, CB_VW = 42, 44, 46, 48, 50, 52
CB_CG = 54

EPS = 1e-6
NEG = -1e30
VMEM_LIMIT = 56 * 1024 * 1024

_NT = (((1,), (1,)), ((), ()))


def _cparams(sem):
    return pltpu.CompilerParams(dimension_semantics=sem, vmem_limit_bytes=VMEM_LIMIT)


def _rms_rows(x, gain):
    return x * lax.rsqrt(jnp.mean(x * x, axis=-1, keepdims=True) + EPS) * gain


def _norm_matmul_kernel(x_ref, g_ref, w_ref, o_ref, h_scr):
    @pl.when(pl.program_id(1) == 0)
    def _():
        h_scr[...] = _rms_rows(x_ref[...], g_ref[...]).astype(BF16)

    o_ref[...] = jnp.dot(h_scr[...], w_ref[...], preferred_element_type=F32)


def _norm_matmul(x, gain, w, layer, tm=1024, tn=512):
    m, k = x.shape
    n = w.shape[2]
    return pl.pallas_call(
        _norm_matmul_kernel,
        grid=(m // tm, n // tn),
        in_specs=[pl.BlockSpec((tm, k), lambda i, j: (i, 0)),
                  pl.BlockSpec((1, k), lambda i, j: (0, 0)),
                  pl.BlockSpec((None, k, tn), lambda i, j: (layer, 0, j))],
        out_specs=pl.BlockSpec((tm, tn), lambda i, j: (i, j)),
        out_shape=jax.ShapeDtypeStruct((m, n), F32),
        scratch_shapes=[pltpu.VMEM((tm, k), BF16)],
        compiler_params=_cparams(("parallel", "arbitrary")),
        name="norm_in_proj",
    )(x, gain.reshape(1, k), w)


def _ffn_up_kernel(x_ref, g_ref, wg_ref, wu_ref, o_ref, h_scr):
    @pl.when(pl.program_id(1) == 0)
    def _():
        h_scr[...] = _rms_rows(x_ref[...], g_ref[...]).astype(BF16)

    h = h_scr[...]
    a = jnp.dot(h, wg_ref[...].astype(BF16), preferred_element_type=F32)
    b = jnp.dot(h, wu_ref[...].astype(BF16), preferred_element_type=F32)
    o_ref[...] = (a * jax.nn.sigmoid(a) * b).astype(BF16)


def _ffn_up(x, gain, wg, wu, layer, tm=1024, tn=512):
    m, k = x.shape
    n = wg.shape[2]
    return pl.pallas_call(
        _ffn_up_kernel,
        grid=(m // tm, n // tn),
        in_specs=[pl.BlockSpec((tm, k), lambda i, j: (i, 0)),
                  pl.BlockSpec((1, k), lambda i, j: (0, 0)),
                  pl.BlockSpec((None, k, tn), lambda i, j: (layer, 0, j)),
                  pl.BlockSpec((None, k, tn), lambda i, j: (layer, 0, j))],
        out_specs=pl.BlockSpec((tm, tn), lambda i, j: (i, j)),
        out_shape=jax.ShapeDtypeStruct((m, n), BF16),
        scratch_shapes=[pltpu.VMEM((tm, k), BF16)],
        compiler_params=_cparams(("parallel", "arbitrary")),
        name="ffn_up",
    )(x, gain.reshape(1, k), wg, wu)


def _matmul_res_kernel(a_ref, w_ref, r_ref, o_ref):
    o_ref[...] = r_ref[...] + jnp.dot(a_ref[...], w_ref[...].astype(BF16), preferred_element_type=F32)


def _matmul_res(a, w, res, layer, tm=1024, tn=256):
    m, k = a.shape
    n = w.shape[2]
    return pl.pallas_call(
        _matmul_res_kernel,
        grid=(m // tm, n // tn),
        in_specs=[pl.BlockSpec((tm, k), lambda i, j: (i, 0)),
                  pl.BlockSpec((None, k, tn), lambda i, j: (layer, 0, j)),
                  pl.BlockSpec((tm, tn), lambda i, j: (i, j))],
        out_specs=pl.BlockSpec((tm, tn), lambda i, j: (i, j)),
        out_shape=jax.ShapeDtypeStruct((m, n), F32),
        compiler_params=_cparams(("parallel", "arbitrary")),
        name="ffn_down",
    )(a, w, res)


def _out_proj_kernel(ya_ref, yb_ref, yc_ref, w_ref, r_ref, o_ref):
    acc = jnp.dot(ya_ref[...], w_ref[0:D_A, :].astype(BF16), preferred_element_type=F32)
    acc += jnp.dot(yb_ref[...], w_ref[D_A:D_A + D_B, :].astype(BF16), preferred_element_type=F32)
    acc += jnp.dot(yc_ref[...], w_ref[D_A + D_B:, :].astype(BF16), preferred_element_type=F32)
    o_ref[...] = r_ref[...] + acc


def _out_proj(ya, yb, yc, w, res, layer, tm=1024, tn=512):
    m = ya.shape[0]
    _, k, n = w.shape
    return pl.pallas_call(
        _out_proj_kernel,
        grid=(m // tm, n // tn),
        in_specs=[pl.BlockSpec((tm, D_A), lambda i, j: (i, 0)),
                  pl.BlockSpec((tm, D_B), lambda i, j: (i, 0)),
                  pl.BlockSpec((tm, D_C), lambda i, j: (i, 0)),
                  pl.BlockSpec((None, k, tn), lambda i, j: (layer, 0, j)),
                  pl.BlockSpec((tm, tn), lambda i, j: (i, j))],
        out_specs=pl.BlockSpec((tm, tn), lambda i, j: (i, j)),
        out_shape=jax.ShapeDtypeStruct((m, n), F32),
        compiler_params=_cparams(("parallel", "arbitrary")),
        name="out_proj",
    )(ya, yb, yc, w, res)


def _final_norm_kernel(x_ref, g_ref, o_ref):
    o_ref[...] = _rms_rows(x_ref[...], g_ref[...])


def _final_norm(x, gain, tm=512):
    m, k = x.shape
    return pl.pallas_call(
        _final_norm_kernel,
        grid=(m // tm,),
        in_specs=[pl.BlockSpec((tm, k), lambda i: (i, 0)),
                  pl.BlockSpec((1, k), lambda i: (0, 0))],
        out_specs=pl.BlockSpec((tm, k), lambda i: (i, 0)),
        out_shape=jax.ShapeDtypeStruct((m, k), F32),
        compiler_params=_cparams(("parallel",)),
        name="final_norm",
    )(x, gain.reshape(1, k))


ATT_TK = 256


def _softmax_block(s, bias, m, l):
    if bias is not None:
        s = s + bias
    m_new = jnp.maximum(m, jnp.max(s, axis=0, keepdims=True))
    alpha = jnp.exp(m - m_new)
    p = jnp.exp(s - m_new)
    return p.astype(BF16), alpha, m_new, alpha * l + jnp.sum(p, axis=0, keepdims=True)


def _attend_range(qall, kb_scr, vt_scr, lo, hi, bias_fn, tail_bias_fn, n_tail, s_scr, p_scr, acc_scr, tk):
    rows = qall.shape[0]

    def scores(b):
        k0 = pl.multiple_of(b * tk, tk)
        return lax.dot_general(kb_scr[pl.ds(k0, tk), :], qall, _NT, preferred_element_type=F32)

    def values(b, p):
        k0 = pl.multiple_of(b * tk, tk)
        return jnp.dot(vt_scr[:, pl.ds(k0, tk)], p, preferred_element_type=F32)

    acc_scr[...] = jnp.zeros_like(acc_scr)
    s_scr[lo % 2] = scores(lo)
    p_scr[(lo + 1) % 2] = jnp.zeros(p_scr.shape[1:], BF16)

    def body(b, carry):
        m, l, alpha_prev = carry
        cur = b % 2
        pv_prev = values(jnp.maximum(b - 1, lo), p_scr[1 - cur])
        p, alpha, m, l = _softmax_block(s_scr[cur], bias_fn(b), m, l)
        acc_scr[...] = alpha_prev * acc_scr[...] + pv_prev
        p_scr[cur] = p
        s_scr[1 - cur] = scores(b + 1)
        return m, l, alpha

    init = (jnp.full((1, rows), NEG, F32), jnp.zeros((1, rows), F32), jnp.ones((1, rows), F32))
    m, l, alpha_prev = lax.fori_loop(lo, hi - n_tail, body, init)
    b0 = hi - n_tail
    s_cur = s_scr[b0 % 2]
    p_prev = p_scr[1 - b0 % 2]
    for t in range(n_tail):
        b = b0 + t
        pv_prev = values(jnp.maximum(b - 1, lo), p_prev)
        s_next = scores(b + 1) if t + 1 < n_tail else None
        p_prev, alpha, m, l = _softmax_block(s_cur, tail_bias_fn(b), m, l)
        acc_scr[...] = alpha_prev * acc_scr[...] + pv_prev
        alpha_prev = alpha
        s_cur = s_next
    acc_scr[...] = alpha_prev * acc_scr[...] + values(hi - 1, p_prev)
    return m, l


def _stage_kv(k_ref, v_ref, kb_scr, vt_scr, seq, tk):
    for c in range(seq // tk):
        rows = slice(c * tk, (c + 1) * tk)
        kb_scr[rows, :] = k_ref[rows, :].astype(BF16)
        vt_scr[:, rows] = v_ref[rows, :].T.astype(BF16)


def _diff_attn_kernel(q_ref, k_ref, v_ref, lq1_ref, lk1_ref, lq2_ref, lk2_ref, gn_ref, o_ref,
                      kb_scr, vt_scr, s_scr, p_scr, acc_scr, *, lam_init, tq, seq):
    h = pl.program_id(1)
    i = pl.program_id(2)
    tk = ATT_TK
    n_diag = tq // tk

    @pl.when(i == 0)
    def _():
        _stage_kv(k_ref, v_ref, kb_scr, vt_scr, seq, tk)

    q = q_ref[...] * (DA_QK_DIM ** -0.5)
    lane = lax.broadcasted_iota(jnp.int32, q.shape, 1)
    qbd = jnp.concatenate([jnp.where(lane < DA_QK_DIM, q, 0.0), jnp.where(lane >= DA_QK_DIM, q, 0.0)],
                          axis=0).astype(BF16)
    tpos = i * tq + lax.broadcasted_iota(jnp.int32, (1, tq), 1)
    kio = lax.broadcasted_iota(jnp.int32, (tk, 1), 0)

    def causal(b):
        keep = jnp.where(b * tk + kio <= tpos, 0.0, NEG)
        return jnp.concatenate([keep, keep], axis=1)

    _, l = _attend_range(qbd, kb_scr, vt_scr, 0, (i + 1) * n_diag, lambda b: None, causal, n_diag,
                         s_scr, p_scr, acc_scr, tk)
    o_t = acc_scr[...] / jnp.maximum(l, 1e-30)
    lam = (jnp.exp(jnp.sum(lq1_ref[...] * lk1_ref[...], axis=-1, keepdims=True))
           - jnp.exp(jnp.sum(lq2_ref[...] * lk2_ref[...], axis=-1, keepdims=True)) + lam_init)
    o = (o_t[:, :tq] - lam * o_t[:, tq:]).T
    y = _rms_rows(o, gn_ref[pl.ds(h, 1), :]) * (1.0 - lam_init)
    o_ref[...] = y.astype(BF16)


def _diff_attn(proj, lq1, lk1, lq2, lk2, gn, layer, batch, seq, tq=256):
    nq = seq // tq
    lam_init = 0.8 - 0.6 * math.exp(-0.3 * layer)
    vec = pl.BlockSpec((1, DA_QK_DIM), lambda b, h, i: (0, 0))
    return pl.pallas_call(
        functools.partial(_diff_attn_kernel, lam_init=lam_init, tq=tq, seq=seq),
        grid=(batch, DA_HEADS, nq),
        in_specs=[pl.BlockSpec((tq, LANE), lambda b, h, i: (b * nq + i, CB_AQ + h)),
                  pl.BlockSpec((seq, LANE), lambda b, h, i: (b, CB_AK + h)),
                  pl.BlockSpec((seq, LANE), lambda b, h, i: (b, CB_AV + h)),
                  vec, vec, vec, vec,
                  pl.BlockSpec((DA_HEADS, DA_V_DIM), lambda b, h, i: (0, 0))],
        out_specs=pl.BlockSpec((tq, LANE), lambda b, h, i: (b * nq + i, h)),
        out_shape=jax.ShapeDtypeStruct((batch * seq, D_A), BF16),
        scratch_shapes=[pltpu.VMEM((seq, LANE), BF16), pltpu.VMEM((DA_V_DIM, seq), BF16),
                        pltpu.VMEM((2, ATT_TK, 2 * tq), F32), pltpu.VMEM((2, ATT_TK, 2 * tq), BF16),
                        pltpu.VMEM((DA_V_DIM, 2 * tq), F32)],
        compiler_params=_cparams(("parallel", "parallel", "arbitrary")),
        name="diff_attn",
    )(proj, proj, proj, lq1.reshape(1, -1), lk1.reshape(1, -1), lq2.reshape(1, -1), lk2.reshape(1, -1), gn)


HG_LEVELS = (64, 32, 16, 8, 4, 2, 1)
HG_SUB = 2 * HG_LEVELS[0]


def _hgrn_consts():
    tb = HG_SUB
    t = np.arange(tb)[:, None]
    u = np.arange(tb)[None, :]
    sums, owns = [], []
    for c in HG_LEVELS:
        mid = (t // (2 * c)) * (2 * c) + c
        second = (t % (2 * c)) >= c
        sums.append(np.where(second, (u >= mid) & (u <= t), (u > t) & (u < mid)))
        owns.append(((t // (2 * c)) == (u // (2 * c))) & second & ((u % (2 * c)) < c))
    sums.append(u <= t)
    return (jnp.asarray(np.concatenate(sums, 0).astype(np.float32), BF16),
            jnp.asarray(np.stack(owns).astype(np.float32)))


def _hgrn_kernel(f_ref, q_ref, i_ref, g_ref, gam_ref, gn_ref, sums_ref, owns_ref, o_ref, st_scr,
                 *, layer, ts):
    h = pl.program_id(1)
    tb = HG_SUB
    nlev = len(HG_LEVELS)

    @pl.when(pl.program_id(2) == 0)
    def _():
        st_scr[...] = jnp.zeros_like(st_scr)

    if layer > 0:
        gam = gam_ref[...]
        e = jnp.exp(gam - jnp.max(gam, axis=0, keepdims=True))
        lb = jnp.sum(e[1:layer + 1], axis=0, keepdims=True) / jnp.sum(e, axis=0, keepdims=True)
        log_lb = jnp.log(lb)
        log_1mlb = jnp.log(1.0 - lb)
    gain = gn_ref[pl.ds(h, 1), :]
    st = st_scr[...]

    for sb in range(ts // tb):
        rows = slice(sb * tb, (sb + 1) * tb)
        z = f_ref[rows, :]
        soft = jnp.log(1.0 + jnp.exp(-jnp.abs(z)))
        logsig = jnp.minimum(z, 0.0) - soft
        logsig_neg = jnp.minimum(-z, 0.0) - soft
        if layer == 0:
            logf = logsig
            key = jnp.exp(logsig_neg)
        else:
            b = log_1mlb + logsig
            logf = jnp.maximum(log_lb, b) + jnp.log(1.0 + jnp.exp(-jnp.abs(log_lb - b)))
            key = (1.0 - lb) * jnp.exp(logsig_neg)
        hi = logf.astype(BF16)
        r1 = logf - hi.astype(F32)
        mid = r1.astype(BF16)
        lo = (r1 - mid.astype(F32)).astype(BF16)
        parts = jnp.dot(sums_ref[...], jnp.concatenate([hi, mid, lo], axis=1), preferred_element_type=F32)

        def expo(i):
            blk = parts[i * tb:(i + 1) * tb]
            return blk[:, 0:HG_DIM] + blk[:, HG_DIM:2 * HG_DIM] + blk[:, 2 * HG_DIM:3 * HG_DIM]

        q = q_ref[rows, :]
        v = i_ref[rows, :]
        vb = v.astype(BF16)
        scores = None
        for i in range(nlev):
            w = jnp.exp(expo(i))
            a = lax.dot_general((q * w).astype(BF16), (key * w).astype(BF16), _NT, preferred_element_type=F32)
            a = a * owns_ref[i]
            scores = a if scores is None else scores + a
        cum = expo(nlev)
        o = jnp.dot(scores.astype(BF16), vb, preferred_element_type=F32)
        o = o + jnp.sum(q * key, axis=-1, keepdims=True) * v
        o = o + lax.dot_general((q * jnp.exp(cum)).astype(BF16), st.astype(BF16), _NT, preferred_element_type=F32)
        gt = g_ref[rows, :]
        o_ref[rows, :] = (_rms_rows(o, gain) * (gt * jax.nn.sigmoid(gt))).astype(BF16)
        last = cum[tb - 1:tb, :]
        ke = (key * jnp.exp(last - cum)).astype(BF16)
        st = st * jnp.exp(last) + jnp.dot(v.T.astype(BF16), ke, preferred_element_type=F32)

    st_scr[...] = st


def _hgrn(proj, gamma, gn, layer, batch, seq, ts=512):
    nt = seq // ts
    depth = gamma.shape[0]
    sums, owns = _hgrn_consts()

    def blk(cb):
        return pl.BlockSpec((ts, LANE), lambda b, h, t: (b * nt + t, cb + h))

    return pl.pallas_call(
        functools.partial(_hgrn_kernel, layer=layer, ts=ts),
        grid=(batch, HG_HEADS, nt),
        in_specs=[blk(CB_BF), blk(CB_BQ), blk(CB_BI), blk(CB_BG),
                  pl.BlockSpec((depth, LANE), lambda b, h, t: (0, h)),
                  pl.BlockSpec((HG_HEADS, HG_DIM), lambda b, h, t: (0, 0)),
                  pl.BlockSpec(sums.shape, lambda b, h, t: (0, 0)),
                  pl.BlockSpec(owns.shape, lambda b, h, t: (0, 0, 0))],
        out_specs=pl.BlockSpec((ts, LANE), lambda b, h, t: (b * nt + t, h)),
        out_shape=jax.ShapeDtypeStruct((batch * seq, D_B), BF16),
        scratch_shapes=[pltpu.VMEM((HG_DIM, HG_DIM), F32)],
        compiler_params=_cparams(("parallel", "parallel", "arbitrary")),
        name="hgrn2",
    )(proj, proj, proj, proj, gamma, gn, sums, owns)


def _gelu_tanh(x):
    return 0.5 * x * (1.0 + jnp.tanh(math.sqrt(2.0 / math.pi) * (x + 0.044715 * (x * x * x))))


def _compress_kernel(xk_ref, xv_ref, pek_ref, pev_ref, w1k_ref, w1v_ref, w2k_ref, w2v_ref, o_ref, *, nblk):
    half = CMP_LEN // 2
    for which, (x_ref, pe_ref, w1_ref, w2_ref) in enumerate(((xk_ref, pek_ref, w1k_ref, w2k_ref),
                                                             (xv_ref, pev_ref, w1v_ref, w2v_ref))):
        top = jnp.zeros((nblk, CMP_HIDDEN), F32)
        bot = jnp.zeros((nblk, CMP_HIDDEN), F32)
        for r in range(half):
            xr = x_ref[pl.ds(r, nblk, stride=CMP_STRIDE), :]
            top += jnp.dot((xr + pe_ref[r:r + 1, :]).astype(BF16),
                           w1_ref[r * NSA_DIM:(r + 1) * NSA_DIM, :].astype(BF16), preferred_element_type=F32)
            bot += jnp.dot((xr + pe_ref[half + r:half + r + 1, :]).astype(BF16),
                           w1_ref[(half + r) * NSA_DIM:(half + r + 1) * NSA_DIM, :].astype(BF16),
                           preferred_element_type=F32)
        pre = top + pltpu.roll(bot, nblk - 1, 0)
        o_ref[which] = jnp.dot(_gelu_tanh(pre).astype(BF16), w2_ref[...].astype(BF16),
                               preferred_element_type=F32)


def _compress(proj, pe_k, pe_v, w1_k, w1_v, w2_k, w2_v, layer, batch, seq):
    nblk = seq // CMP_STRIDE
    g = NSA_KV_HEADS

    def per_layer(*shape):
        return pl.BlockSpec((None,) + shape, lambda b, j: (layer,) + (0,) * len(shape))

    return pl.pallas_call(
        functools.partial(_compress_kernel, nblk=nblk),
        grid=(batch, g),
        in_specs=[pl.BlockSpec((seq, LANE), lambda b, j: (b, CB_KC + j)),
                  pl.BlockSpec((seq, LANE), lambda b, j: (b, CB_VC + j)),
                  per_layer(CMP_LEN, NSA_DIM), per_layer(CMP_LEN, NSA_DIM),
                  per_layer(CMP_LEN * NSA_DIM, CMP_HIDDEN), per_layer(CMP_LEN * NSA_DIM, CMP_HIDDEN),
                  per_layer(CMP_HIDDEN, NSA_DIM), per_layer(CMP_HIDDEN, NSA_DIM)],
        out_specs=pl.BlockSpec((None, 2, None, nblk, NSA_DIM), lambda b, j: (b, 0, j, 0, 0)),
        out_shape=jax.ShapeDtypeStruct((batch, 2, g, nblk, NSA_DIM), F32),
        compiler_params=_cparams(("parallel", "parallel")),
        name="nsa_compress",
    )(proj, proj, pe_k, pe_v, w1_k, w1_v, w2_k, w2_v)


def _nsa_kernel(q_ref, kc_ref, vc_ref, ks_ref, vs_ref, kw_ref, vw_ref, gl_ref, o_ref,
                ksb_scr, vst_scr, kwb_scr, vwt_scr, selb_scr, s_scr, p_scr, sw_scr, pw_scr, accs_scr, accw_scr, *, tq, seq):
    grp = pl.program_id(1)
    qi = pl.program_id(2)
    nj = NSA_GROUP
    tk = ATT_TK
    n_diag = tq // tk
    n_sel = seq // SLC_BLOCK
    ncmp_pad = seq // CMP_STRIDE

    @pl.when(qi == 0)
    def _():
        _stage_kv(ks_ref, vs_ref, ksb_scr, vst_scr, seq, tk)
        _stage_kv(kw_ref, vw_ref, kwb_scr, vwt_scr, seq, tk)

    qall = jnp.concatenate([q_ref[:, j * NSA_DIM:(j + 1) * NSA_DIM] for j in range(nj)], axis=0)
    qall = (qall * (NSA_DIM ** -0.5)).astype(BF16)
    tpos = qi * tq + lax.broadcasted_iota(jnp.int32, (1, tq), 1)

    def rep(x):
        return jnp.concatenate([x] * nj, axis=1)

    sc = lax.dot_general(kc_ref[...].astype(BF16), qall, _NT, preferred_element_type=F32)
    n_end = lax.broadcasted_iota(jnp.int32, (ncmp_pad, 1), 0) * CMP_STRIDE + (CMP_LEN - 1)
    cvis = rep(n_end <= tpos)
    m = jnp.max(jnp.where(cvis, sc, NEG), axis=0, keepdims=True)
    p = jnp.exp(jnp.where(cvis, sc - m, NEG))
    p = p / jnp.maximum(jnp.sum(p, axis=0, keepdims=True), 1e-30)
    o_cmp = jnp.dot(vc_ref[...].T.astype(BF16), p.astype(BF16), preferred_element_type=F32)
    psum = p[:, 0:tq] + p[:, tq:2 * tq] + p[:, 2 * tq:3 * tq]
    cn = lax.broadcasted_iota(jnp.int32, (n_sel, ncmp_pad), 1) * CMP_STRIDE
    sb = lax.broadcasted_iota(jnp.int32, (n_sel, ncmp_pad), 0) * SLC_BLOCK
    overlap_t = jnp.where((cn < sb + SLC_BLOCK) & (cn + CMP_LEN > sb), 1.0, 0.0).astype(BF16)
    hi = psum.astype(BF16)
    r1 = psum - hi.astype(F32)
    mid = r1.astype(BF16)
    lo = (r1 - mid.astype(F32)).astype(BF16)
    imp = (jnp.dot(overlap_t, hi, preferred_element_type=F32)
           + jnp.dot(overlap_t, mid, preferred_element_type=F32)
           + jnp.dot(overlap_t, lo, preferred_element_type=F32))

    blk = lax.broadcasted_iota(jnp.int32, (n_sel, 1), 0)
    cur = tpos // SLC_BLOCK
    forced = (blk == 0) | (blk == cur) | (blk == cur - 1)
    valid = blk * SLC_BLOCK <= tpos
    score = jnp.where(valid, imp + jnp.where(forced, FORCE_BONUS, 0.0), -jnp.inf)
    rank = jnp.zeros((n_sel, tq), F32)
    for i in range(n_sel):
        ci = score[i:i + 1, :]
        rank += jnp.where(blk > i, jnp.where(ci >= score, 1.0, 0.0), jnp.where(ci > score, 1.0, 0.0))
    sel_mask = jnp.where(rank < float(min(SLC_TOPK, n_sel)), 0.0, NEG)
    for r in range(n_sel):
        selb_scr[r * SLC_BLOCK:(r + 1) * SLC_BLOCK, :] = jnp.broadcast_to(sel_mask[r:r + 1, :], (SLC_BLOCK, tq))

    kio = lax.broadcasted_iota(jnp.int32, (tk, 1), 0)

    def sel_bias(kb):
        return selb_scr[pl.ds(pl.multiple_of(kb * tk, tk), tk), :]

    _, l_s = _attend_range(qall, ksb_scr, vst_scr, 0, (qi + 1) * n_diag, lambda b: rep(sel_bias(b)),
                           lambda b: rep(jnp.where(b * tk + kio <= tpos, sel_bias(b), NEG)), n_diag,
                           s_scr, p_scr, accs_scr, tk)
    o_slc = accs_scr[...] / jnp.maximum(l_s, 1e-30)

    def win_bias(b):
        dist = tpos - (b * tk + kio)
        return rep(jnp.where((dist >= 0) & (dist < WINDOW), 0.0, NEG))

    m_w, l_w = _attend_range(qall, kwb_scr, vwt_scr, jnp.maximum(qi * n_diag - WINDOW // tk, 0),
                             (qi + 1) * n_diag, win_bias, win_bias, 1, sw_scr, pw_scr, accw_scr, tk)
    n_pad = rep(jnp.maximum(WINDOW - 1 - tpos, 0)).astype(F32)
    m_p = jnp.where(n_pad > 0.0, jnp.maximum(m_w, 0.0), m_w)
    a_p = jnp.exp(m_w - m_p)
    l_p = jnp.where(n_pad > 0.0, n_pad * jnp.exp(-jnp.maximum(m_p, 0.0)), 0.0)
    o_win = (accw_scr[...] * a_p) / jnp.maximum(l_w * a_p + l_p, 1e-30)

    gate = jax.nn.sigmoid(gl_ref[...].T)
    for j in range(nj):
        def grow(br):
            c0 = 3 * j + br
            c1 = 3 * (nj + j) + br
            return jnp.where(grp == 0, gate[c0:c0 + 1, :], gate[c1:c1 + 1, :])
        lanes = slice(j * tq, (j + 1) * tq)
        y_t = grow(0) * o_cmp[:, lanes] + grow(1) * o_slc[:, lanes] + grow(2) * o_win[:, lanes]
        o_ref[:, j * NSA_DIM:(j + 1) * NSA_DIM] = y_t.T.astype(BF16)


def _nsa(proj, cmp, batch, seq, tq=256):
    nq = seq // tq
    g = NSA_KV_HEADS
    gw = NSA_GROUP * NSA_DIM
    ncmp_pad = seq // CMP_STRIDE

    def kv(cb):
        return pl.BlockSpec((seq, LANE), lambda b, j, i: (b, cb + j))

    def cmp_spec(which):
        return pl.BlockSpec((None, None, None, ncmp_pad, NSA_DIM), lambda b, j, i: (b, which, j, 0, 0))

    kb_scr = pltpu.VMEM((seq, LANE), BF16)
    vt_scr = pltpu.VMEM((NSA_DIM, seq), BF16)
    acc_scr = pltpu.VMEM((NSA_DIM, NSA_GROUP * tq), F32)
    s_scr = pltpu.VMEM((2, ATT_TK, NSA_GROUP * tq), F32)
    p_scr = pltpu.VMEM((2, ATT_TK, NSA_GROUP * tq), BF16)
    return pl.pallas_call(
        functools.partial(_nsa_kernel, tq=tq, seq=seq),
        grid=(batch, g, nq),
        in_specs=[pl.BlockSpec((tq, gw), lambda b, j, i: (b * nq + i, CB_CQ * LANE // gw + j)),
                  cmp_spec(0), cmp_spec(1),
                  kv(CB_KS), kv(CB_VS), kv(CB_KW), kv(CB_VW),
                  pl.BlockSpec((tq, LANE), lambda b, j, i: (b * nq + i, CB_CG))],
        out_specs=pl.BlockSpec((tq, gw), lambda b, j, i: (b * nq + i, j)),
        out_shape=jax.ShapeDtypeStruct((batch * seq, D_C), BF16),
        scratch_shapes=[kb_scr, vt_scr, kb_scr, vt_scr, pltpu.VMEM((seq, tq), F32),
                        s_scr, p_scr, s_scr, p_scr, acc_scr, acc_scr],
        compiler_params=_cparams(("parallel", "parallel", "arbitrary")),
        name="nsa",
    )(proj, cmp, cmp, proj, proj, proj, proj, proj)


def kernel(x, attn_norm, w_in, da_lam_q1, da_lam_k1, da_lam_q2, da_lam_k2, da_norm, hg_gamma, hg_norm,
           nsa_pe_k, nsa_pe_v, nsa_ck_w1, nsa_ck_w2, nsa_cv_w1, nsa_cv_w2, w_out, ffn_norm,
           w_gate, w_up, w_down, final_norm):
    batch, seq, d = x.shape
    depth = w_in.shape[0]
    assert (CB_CQ * LANE) % (NSA_GROUP * NSA_DIM) == 0 and CB_CG * LANE + NSA_HEADS * 3 == D_IN
    xf = x.reshape(batch * seq, d)
    w_in_b = jnp.pad(w_in, ((0, 0), (0, 0), (0, D_IN_PAD - D_IN))).astype(BF16)
    for l in range(depth):
        proj = _norm_matmul(xf, attn_norm[l], w_in_b, l)
        ya = _diff_attn(proj, da_lam_q1[l], da_lam_k1[l], da_lam_q2[l], da_lam_k2[l], da_norm[l],
                        l, batch, seq)
        yb = _hgrn(proj, hg_gamma, hg_norm[l], l, batch, seq)
        cmp = _compress(proj, nsa_pe_k, nsa_pe_v, nsa_ck_w1, nsa_cv_w1, nsa_ck_w2, nsa_cv_w2, l, batch, seq)
        yc = _nsa(proj, cmp, batch, seq)
        xf = _out_proj(ya, yb, yc, w_out, xf, l)
        u = _ffn_up(xf, ffn_norm[l], w_gate, w_up, l)
        xf = _matmul_res(u, w_down, xf, l)
    return _final_norm(xf, final_norm).reshape(batch, seq, d)
```

```python
import functools
import math

import jax
import jax.numpy as jnp
import numpy as np
from jax import lax
from jax.experimental import pallas as pl
from jax.experimental.pallas import tpu as pltpu

F32 = jnp.float32
BF16 = jnp.bfloat16

D_MODEL = 2048
DA_HEADS = 4
DA_QK_DIM = 64
DA_V_DIM = 128
HG_HEADS = 6
HG_DIM = 128
NSA_HEADS = 6
NSA_KV_HEADS = 2
NSA_GROUP = NSA_HEADS // NSA_KV_HEADS
NSA_DIM = 128
CMP_LEN = 32
CMP_STRIDE = 16
CMP_HIDDEN = 256
SLC_BLOCK = 64
SLC_TOPK = 16
WINDOW = 512
FORCE_BONUS = 1.0e4
D_A = DA_HEADS * DA_V_DIM
D_B = HG_HEADS * HG_DIM
D_C = NSA_HEADS * NSA_DIM
D_FF = ((8 * D_MODEL // 3 + 255) // 256) * 256
D_IN = 3 * D_A + 4 * D_B + D_C + 6 * NSA_KV_HEADS * NSA_DIM + NSA_HEADS * 3

LANE = 128
CB_AQ, CB_AK, CB_AV = 0, 4, 8
CB_BF, CB_BQ, CB_BI, CB_BG = 12, 18, 24, 30
CB_CQ = 36
CB_KC, CB_VC, CB_KS, CB_VS, CB_KW, CB_VW = 42, 44, 46, 48, 50, 52
D_IN_MAIN = 54 * LANE
D_IN_PAD = D_IN_MAIN

EPS = 1e-6
NEG = -1e30
VMEM_LIMIT = 56 * 1024 * 1024

_NT = (((1,), (1,)), ((), ()))


def _cparams(sem):
    return pltpu.CompilerParams(dimension_semantics=sem, vmem_limit_bytes=VMEM_LIMIT)


def _rms_rows(x, gain):
    return x * lax.rsqrt(jnp.mean(x * x, axis=-1, keepdims=True) + EPS) * gain


def _norm_matmul_kernel(x_ref, g_ref, w_ref, wt_ref, o_ref, ot_ref, h_scr):
    @pl.when(pl.program_id(1) == 0)
    def _():
        h = _rms_rows(x_ref[...], g_ref[...]).astype(BF16)
        h_scr[...] = h
        ot_ref[...] = jnp.dot(h, wt_ref[...].astype(BF16), preferred_element_type=F32)

    o_ref[...] = jnp.dot(h_scr[...], w_ref[...].astype(BF16), preferred_element_type=F32)


def _norm_matmul(x, gain, w, w_tail, layer, tm=1024, tn=768):
    m, k = x.shape
    nt = w_tail.shape[2]
    return pl.pallas_call(
        _norm_matmul_kernel,
        grid=(m // tm, D_IN_MAIN // tn),
        in_specs=[pl.BlockSpec((tm, k), lambda i, j: (i, 0)),
                  pl.BlockSpec((1, k), lambda i, j: (0, 0)),
                  pl.BlockSpec((None, k, tn), lambda i, j: (layer, 0, j)),
                  pl.BlockSpec((None, k, nt), lambda i, j: (layer, 0, 0))],
        out_specs=[pl.BlockSpec((tm, tn), lambda i, j: (i, j)),
                   pl.BlockSpec((tm, nt), lambda i, j: (i, 0))],
        out_shape=[jax.ShapeDtypeStruct((m, D_IN_MAIN), F32), jax.ShapeDtypeStruct((m, nt), F32)],
        scratch_shapes=[pltpu.VMEM((tm, k), BF16)],
        compiler_params=_cparams(("parallel", "arbitrary")),
        name="norm_in_proj",
    )(x, gain.reshape(1, k), w, w_tail)


def _ffn_up_kernel(x_ref, g_ref, wg_ref, wu_ref, o_ref, h_scr):
    @pl.when(pl.program_id(1) == 0)
    def _():
        h_scr[...] = _rms_rows(x_ref[...], g_ref[...]).astype(BF16)

    h = h_scr[...]
    a = jnp.dot(h, wg_ref[...].astype(BF16), preferred_element_type=F32)
    b = jnp.dot(h, wu_ref[...].astype(BF16), preferred_element_type=F32)
    o_ref[...] = (a * jax.nn.sigmoid(a) * b).astype(BF16)


def _ffn_up(x, gain, wg, wu, layer, tm=1024, tn=512):
    m, k = x.shape
    n = wg.shape[2]
    return pl.pallas_call(
        _ffn_up_kernel,
        grid=(m // tm, n // tn),
        in_specs=[pl.BlockSpec((tm, k), lambda i, j: (i, 0)),
                  pl.BlockSpec((1, k), lambda i, j: (0, 0)),
                  pl.BlockSpec((None, k, tn), lambda i, j: (layer, 0, j)),
                  pl.BlockSpec((None, k, tn), lambda i, j: (layer, 0, j))],
        out_specs=pl.BlockSpec((tm, tn), lambda i, j: (i, j)),
        out_shape=jax.ShapeDtypeStruct((m, n), BF16),
        scratch_shapes=[pltpu.VMEM((tm, k), BF16)],
        compiler_params=_cparams(("parallel", "arbitrary")),
        name="ffn_up",
    )(x, gain.reshape(1, k), wg, wu)


def _matmul_res_kernel(a_ref, w_ref, r_ref, o_ref):
    o_ref[...] = r_ref[...] + jnp.dot(a_ref[...], w_ref[...].astype(BF16), preferred_element_type=F32)


def _matmul_res(a, w, res, layer, tm=1024, tn=256):
    m, k = a.shape
    n = w.shape[2]
    return pl.pallas_call(
        _matmul_res_kernel,
        grid=(m // tm, n // tn),
        in_specs=[pl.BlockSpec((tm, k), lambda i, j: (i, 0)),
                  pl.BlockSpec((None, k, tn), lambda i, j: (layer, 0, j)),
                  pl.BlockSpec((tm, tn), lambda i, j: (i, j))],
        out_specs=pl.BlockSpec((tm, tn), lambda i, j: (i, j)),
        out_shape=jax.ShapeDtypeStruct((m, n), F32),
        compiler_params=_cparams(("parallel", "arbitrary")),
        name="ffn_down",
    )(a, w, res)


def _out_proj_kernel(ya_ref, yb_ref, yc_ref, w_ref, r_ref, o_ref):
    acc = jnp.dot(ya_ref[...], w_ref[0:D_A, :].astype(BF16), preferred_element_type=F32)
    acc += jnp.dot(yb_ref[...], w_ref[D_A:D_A + D_B, :].astype(BF16), preferred_element_type=F32)
    acc += jnp.dot(yc_ref[...], w_ref[D_A + D_B:, :].astype(BF16), preferred_element_type=F32)
    o_ref[...] = r_ref[...] + acc


def _out_proj(ya, yb, yc, w, res, layer, tm=1024, tn=512):
    m = ya.shape[0]
    _, k, n = w.shape
    return pl.pallas_call(
        _out_proj_kernel,
        grid=(m // tm, n // tn),
        in_specs=[pl.BlockSpec((tm, D_A), lambda i, j: (i, 0)),
                  pl.BlockSpec((tm, D_B), lambda i, j: (i, 0)),
                  pl.BlockSpec((tm, D_C), lambda i, j: (i, 0)),
                  pl.BlockSpec((None, k, tn), lambda i, j: (layer, 0, j)),
                  pl.BlockSpec((tm, tn), lambda i, j: (i, j))],
        out_specs=pl.BlockSpec((tm, tn), lambda i, j: (i, j)),
        out_shape=jax.ShapeDtypeStruct((m, n), F32),
        compiler_params=_cparams(("parallel", "arbitrary")),
        name="out_proj",
    )(ya, yb, yc, w, res)


def _final_norm_kernel(x_ref, g_ref, o_ref):
    o_ref[...] = _rms_rows(x_ref[...], g_ref[...])


def _final_norm(x, gain, tm=512):
    m, k = x.shape
    return pl.pallas_call(
        _final_norm_kernel,
        grid=(m // tm,),
        in_specs=[pl.BlockSpec((tm, k), lambda i: (i, 0)),
                  pl.BlockSpec((1, k), lambda i: (0, 0))],
        out_specs=pl.BlockSpec((tm, k), lambda i: (i, 0)),
        out_shape=jax.ShapeDtypeStruct((m, k), F32),
        compiler_params=_cparams(("parallel",)),
        name="final_norm",
    )(x, gain.reshape(1, k))


ATT_TK = 256


def _softmax_block(s, bias, m, l):
    if bias is not None:
        s = s + bias
    m_new = jnp.maximum(m, jnp.max(s, axis=0, keepdims=True))
    alpha = jnp.exp(m - m_new)
    p = jnp.exp(s - m_new)
    return p.astype(BF16), alpha, m_new, alpha * l + jnp.sum(p, axis=0, keepdims=True)


def _attend_range(qall, kb_scr, vt_scr, lo, hi, bias_fn, tail_bias_fn, n_tail, s_scr, p_scr, acc_scr, tk):
    rows = qall.shape[0]

    def scores(b):
        k0 = pl.multiple_of(b * tk, tk)
        return lax.dot_general(kb_scr[pl.ds(k0, tk), :], qall, _NT, preferred_element_type=F32)

    def values(b, p):
        k0 = pl.multiple_of(b * tk, tk)
        return jnp.dot(vt_scr[:, pl.ds(k0, tk)], p, preferred_element_type=F32)

    def stage(b, s_cur, p_prev, m, l, alpha_prev, bias, want_next):
        pv_prev = values(jnp.maximum(b - 1, lo), p_prev)
        s_next = scores(b + 1) if want_next else None
        p, alpha, m, l = _softmax_block(s_cur, bias, m, l)
        acc_scr[...] = alpha_prev * acc_scr[...] + pv_prev
        return s_next, p, m, l, alpha

    acc_scr[...] = jnp.zeros_like(acc_scr)
    s_scr[...] = scores(lo)
    p_scr[...] = jnp.zeros(p_scr.shape, BF16)
    n_main = hi - n_tail - lo
    odd = n_main % 2

    def one(i, carry):
        m, l, alpha_prev = carry
        b = lo + i
        s_next, p, m, l, alpha = stage(b, s_scr[...], p_scr[...], m, l, alpha_prev, bias_fn(b), True)
        s_scr[...] = s_next
        p_scr[...] = p
        return m, l, alpha

    def two(i, carry):
        m, l, alpha_prev = carry
        b = lo + odd + 2 * i
        s_mid, p_mid, m, l, alpha = stage(b, s_scr[...], p_scr[...], m, l, alpha_prev, bias_fn(b), True)
        s_next, p, m, l, alpha = stage(b + 1, s_mid, p_mid, m, l, alpha, bias_fn(b + 1), True)
        s_scr[...] = s_next
        p_scr[...] = p
        return m, l, alpha

    carry = (jnp.full((1, rows), NEG, F32), jnp.zeros((1, rows), F32), jnp.ones((1, rows), F32))
    carry = lax.fori_loop(0, odd, one, carry)
    m, l, alpha_prev = lax.fori_loop(0, n_main // 2, two, carry)
    s_cur = s_scr[...]
    p_prev = p_scr[...]
    for t in range(n_tail):
        b = hi - n_tail + t
        s_cur, p_prev, m, l, alpha_prev = stage(b, s_cur, p_prev, m, l, alpha_prev, tail_bias_fn(b),
                                                t + 1 < n_tail)
    acc_scr[...] = alpha_prev * acc_scr[...] + values(hi - 1, p_prev)
    return m, l


def _stage_kv(k_ref, v_ref, kb_scr, vt_scr, seq, tk):
    for c in range(seq // tk):
        rows = slice(c * tk, (c + 1) * tk)
        kb_scr[rows, :] = k_ref[rows, :].astype(BF16)
        vt_scr[:, rows] = v_ref[rows, :].T.astype(BF16)


def _diff_attn_kernel(q_ref, k_ref, v_ref, lq1_ref, lk1_ref, lq2_ref, lk2_ref, gn_ref, o_ref,
                      kb_scr, vt_scr, s_scr, p_scr, acc_scr, *, lam_init, tq, seq):
    h = pl.program_id(1)
    i = pl.program_id(2)
    tk = ATT_TK
    n_diag = tq // tk

    @pl.when(i == 0)
    def _():
        _stage_kv(k_ref, v_ref, kb_scr, vt_scr, seq, tk)

    q = q_ref[...] * (DA_QK_DIM ** -0.5)
    lane = lax.broadcasted_iota(jnp.int32, q.shape, 1)
    qbd = jnp.concatenate([jnp.where(lane < DA_QK_DIM, q, 0.0), jnp.where(lane >= DA_QK_DIM, q, 0.0)],
                          axis=0).astype(BF16)
    tpos = i * tq + lax.broadcasted_iota(jnp.int32, (1, tq), 1)
    kio = lax.broadcasted_iota(jnp.int32, (tk, 1), 0)

    def causal(b):
        keep = jnp.where(b * tk + kio <= tpos, 0.0, NEG)
        return jnp.concatenate([keep, keep], axis=1)

    _, l = _attend_range(qbd, kb_scr, vt_scr, 0, (i + 1) * n_diag, lambda b: None, causal, n_diag,
                         s_scr, p_scr, acc_scr, tk)
    o_t = acc_scr[...] / jnp.maximum(l, 1e-30)
    lam = (jnp.exp(jnp.sum(lq1_ref[...] * lk1_ref[...], axis=-1, keepdims=True))
           - jnp.exp(jnp.sum(lq2_ref[...] * lk2_ref[...], axis=-1, keepdims=True)) + lam_init)
    o = (o_t[:, :tq] - lam * o_t[:, tq:]).T
    y = _rms_rows(o, gn_ref[pl.ds(h, 1), :]) * (1.0 - lam_init)
    o_ref[...] = y.astype(BF16)


def _diff_attn(proj, lq1, lk1, lq2, lk2, gn, layer, batch, seq, tq=256):
    nq = seq // tq
    lam_init = 0.8 - 0.6 * math.exp(-0.3 * layer)
    vec = pl.BlockSpec((1, DA_QK_DIM), lambda b, h, i: (0, 0))
    return pl.pallas_call(
        functools.partial(_diff_attn_kernel, lam_init=lam_init, tq=tq, seq=seq),
        grid=(batch, DA_HEADS, nq),
        in_specs=[pl.BlockSpec((tq, LANE), lambda b, h, i: (b * nq + i, CB_AQ + h)),
                  pl.BlockSpec((seq, LANE), lambda b, h, i: (b, CB_AK + h)),
                  pl.BlockSpec((seq, LANE), lambda b, h, i: (b, CB_AV + h)),
                  vec, vec, vec, vec,
                  pl.BlockSpec((DA_HEADS, DA_V_DIM), lambda b, h, i: (0, 0))],
        out_specs=pl.BlockSpec((tq, LANE), lambda b, h, i: (b * nq + i, h)),
        out_shape=jax.ShapeDtypeStruct((batch * seq, D_A), BF16),
        scratch_shapes=[pltpu.VMEM((seq, LANE), BF16), pltpu.VMEM((DA_V_DIM, seq), BF16),
                        pltpu.VMEM((ATT_TK, 2 * tq), F32), pltpu.VMEM((ATT_TK, 2 * tq), BF16),
                        pltpu.VMEM((DA_V_DIM, 2 * tq), F32)],
        compiler_params=_cparams(("parallel", "parallel", "arbitrary")),
        name="diff_attn",
    )(proj, proj, proj, lq1.reshape(1, -1), lk1.reshape(1, -1), lq2.reshape(1, -1), lk2.reshape(1, -1), gn)


HG_LEVELS = (64, 32, 16, 8, 4, 2, 1)
HG_SUB = 2 * HG_LEVELS[0]


def _hgrn_consts():
    tb = HG_SUB
    t = np.arange(tb)[:, None]
    u = np.arange(tb)[None, :]
    sums, owns = [], []
    for c in HG_LEVELS:
        mid = (t // (2 * c)) * (2 * c) + c
        second = (t % (2 * c)) >= c
        sums.append(np.where(second, (u >= mid) & (u <= t), (u > t) & (u < mid)))
        owns.append(((t // (2 * c)) == (u // (2 * c))) & second & ((u % (2 * c)) < c))
    sums.append(u <= t)
    return (jnp.asarray(np.concatenate(sums, 0).astype(np.float32), BF16),
            jnp.asarray(np.stack(owns).astype(np.float32)))


def _hgrn_kernel(f_ref, q_ref, i_ref, g_ref, gam_ref, gn_ref, sums_ref, owns_ref, o_ref, st_scr,
                 *, layer, ts):
    h = pl.program_id(1)
    tb = HG_SUB
    nlev = len(HG_LEVELS)

    @pl.when(pl.program_id(2) == 0)
    def _():
        st_scr[...] = jnp.zeros_like(st_scr)

    if layer > 0:
        gam = gam_ref[...]
        e = jnp.exp(gam - jnp.max(gam, axis=0, keepdims=True))
        lb = jnp.sum(e[1:layer + 1], axis=0, keepdims=True) / jnp.sum(e, axis=0, keepdims=True)
        log_lb = jnp.log(lb)
        log_1mlb = jnp.log(1.0 - lb)
    gain = gn_ref[pl.ds(h, 1), :]
    st = st_scr[...]

    for sb in range(ts // tb):
        rows = slice(sb * tb, (sb + 1) * tb)
        z = f_ref[rows, :]
        soft = jnp.log(1.0 + jnp.exp(-jnp.abs(z)))
        logsig = jnp.minimum(z, 0.0) - soft
        logsig_neg = jnp.minimum(-z, 0.0) - soft
        if layer == 0:
            logf = logsig
            key = jnp.exp(logsig_neg)
        else:
            b = log_1mlb + logsig
            logf = jnp.maximum(log_lb, b) + jnp.log(1.0 + jnp.exp(-jnp.abs(log_lb - b)))
            key = (1.0 - lb) * jnp.exp(logsig_neg)
        hi = logf.astype(BF16)
        mid = (logf - hi.astype(F32)).astype(BF16)
        parts = jnp.dot(sums_ref[...], jnp.concatenate([hi, mid], axis=1), preferred_element_type=F32)

        def expo(i):
            blk = parts[i * tb:(i + 1) * tb]
            return blk[:, 0:HG_DIM] + blk[:, HG_DIM:2 * HG_DIM]

        q = q_ref[rows, :]
        v = i_ref[rows, :]
        vb = v.astype(BF16)
        scores = None
        for i in range(nlev):
            w = jnp.exp(expo(i))
            a = lax.dot_general((q * w).astype(BF16), (key * w).astype(BF16), _NT, preferred_element_type=F32)
            a = a * owns_ref[i]
            scores = a if scores is None else scores + a
        cum = expo(nlev)
        o = jnp.dot(scores.astype(BF16), vb, preferred_element_type=F32)
        o = o + jnp.sum(q * key, axis=-1, keepdims=True) * v
        o = o + lax.dot_general((q * jnp.exp(cum)).astype(BF16), st.astype(BF16), _NT, preferred_element_type=F32)
        gt = g_ref[rows, :]
        o_ref[rows, :] = (_rms_rows(o, gain) * (gt * jax.nn.sigmoid(gt))).astype(BF16)
        last = cum[tb - 1:tb, :]
        ke = (key * jnp.exp(last - cum)).astype(BF16)
        st = st * jnp.exp(last) + jnp.dot(v.T.astype(BF16), ke, preferred_element_type=F32)

    st_scr[...] = st


def _hgrn(proj, gamma, gn, layer, batch, seq, ts=512):
    nt = seq // ts
    depth = gamma.shape[0]
    sums, owns = _hgrn_consts()

    def blk(cb):
        return pl.BlockSpec((ts, LANE), lambda b, h, t: (b * nt + t, cb + h))

    return pl.pallas_call(
        functools.partial(_hgrn_kernel, layer=layer, ts=ts),
        grid=(batch, HG_HEADS, nt),
        in_specs=[blk(CB_BF), blk(CB_BQ), blk(CB_BI), blk(CB_BG),
                  pl.BlockSpec((depth, LANE), lambda b, h, t: (0, h)),
                  pl.BlockSpec((HG_HEADS, HG_DIM), lambda b, h, t: (0, 0)),
                  pl.BlockSpec(sums.shape, lambda b, h, t: (0, 0)),
                  pl.BlockSpec(owns.shape, lambda b, h, t: (0, 0, 0))],
        out_specs=pl.BlockSpec((ts, LANE), lambda b, h, t: (b * nt + t, h)),
        out_shape=jax.ShapeDtypeStruct((batch * seq, D_B), BF16),
        scratch_shapes=[pltpu.VMEM((HG_DIM, HG_DIM), F32)],
        compiler_params=_cparams(("parallel", "parallel", "arbitrary")),
        name="hgrn2",
    )(proj, proj, proj, proj, gamma, gn, sums, owns)


def _gelu_tanh(x):
    return 0.5 * x * (1.0 + jnp.tanh(math.sqrt(2.0 / math.pi) * (x + 0.044715 * (x * x * x))))


def _compress_kernel(xk_ref, xv_ref, pek_ref, pev_ref, w1k_ref, w1v_ref, w2k_ref, w2v_ref, o_ref, *, nblk):
    half = CMP_LEN // 2
    for which, (x_ref, pe_ref, w1_ref, w2_ref) in enumerate(((xk_ref, pek_ref, w1k_ref, w2k_ref),
                                                             (xv_ref, pev_ref, w1v_ref, w2v_ref))):
        top = jnp.zeros((nblk, CMP_HIDDEN), F32)
        bot = jnp.zeros((nblk, CMP_HIDDEN), F32)
        for r in range(half):
            xr = x_ref[pl.ds(r, nblk, stride=CMP_STRIDE), :]
            top += jnp.dot((xr + pe_ref[r:r + 1, :]).astype(BF16),
                           w1_ref[r * NSA_DIM:(r + 1) * NSA_DIM, :].astype(BF16), preferred_element_type=F32)
            bot += jnp.dot((xr + pe_ref[half + r:half + r + 1, :]).astype(BF16),
                           w1_ref[(half + r) * NSA_DIM:(half + r + 1) * NSA_DIM, :].astype(BF16),
                           preferred_element_type=F32)
        pre = top + pltpu.roll(bot, nblk - 1, 0)
        o_ref[which] = jnp.dot(_gelu_tanh(pre).astype(BF16), w2_ref[...].astype(BF16),
                               preferred_element_type=F32)


def _compress(proj, pe_k, pe_v, w1_k, w1_v, w2_k, w2_v, layer, batch, seq):
    nblk = seq // CMP_STRIDE
    g = NSA_KV_HEADS

    def per_layer(*shape):
        return pl.BlockSpec((None,) + shape, lambda b, j: (layer,) + (0,) * len(shape))

    return pl.pallas_call(
        functools.partial(_compress_kernel, nblk=nblk),
        grid=(batch, g),
        in_specs=[pl.BlockSpec((seq, LANE), lambda b, j: (b, CB_KC + j)),
                  pl.BlockSpec((seq, LANE), lambda b, j: (b, CB_VC + j)),
                  per_layer(CMP_LEN, NSA_DIM), per_layer(CMP_LEN, NSA_DIM),
                  per_layer(CMP_LEN * NSA_DIM, CMP_HIDDEN), per_layer(CMP_LEN * NSA_DIM, CMP_HIDDEN),
                  per_layer(CMP_HIDDEN, NSA_DIM), per_layer(CMP_HIDDEN, NSA_DIM)],
        out_specs=pl.BlockSpec((None, 2, None, nblk, NSA_DIM), lambda b, j: (b, 0, j, 0, 0)),
        out_shape=jax.ShapeDtypeStruct((batch, 2, g, nblk, NSA_DIM), F32),
        compiler_params=_cparams(("parallel", "parallel")),
        name="nsa_compress",
    )(proj, proj, pe_k, pe_v, w1_k, w1_v, w2_k, w2_v)


def _nsa_kernel(q_ref, kc_ref, vc_ref, ks_ref, vs_ref, kw_ref, vw_ref, gl_ref, o_ref,
                ksb_scr, vst_scr, kwb_scr, vwt_scr, selb_scr, s_scr, p_scr, sw_scr, pw_scr, accs_scr, accw_scr, *, tq, seq):
    grp = pl.program_id(1)
    qi = pl.program_id(2)
    nj = NSA_GROUP
    tk = ATT_TK
    n_diag = tq // tk
    n_sel = seq // SLC_BLOCK
    ncmp_pad = seq // CMP_STRIDE

    @pl.when(qi == 0)
    def _():
        _stage_kv(ks_ref, vs_ref, ksb_scr, vst_scr, seq, tk)
        _stage_kv(kw_ref, vw_ref, kwb_scr, vwt_scr, seq, tk)

    qall = jnp.concatenate([q_ref[:, j * NSA_DIM:(j + 1) * NSA_DIM] for j in range(nj)], axis=0)
    qall = (qall * (NSA_DIM ** -0.5)).astype(BF16)
    tpos = qi * tq + lax.broadcasted_iota(jnp.int32, (1, tq), 1)

    def rep(x):
        return jnp.concatenate([x] * nj, axis=1)

    sc = lax.dot_general(kc_ref[...].astype(BF16), qall, _NT, preferred_element_type=F32)
    n_end = lax.broadcasted_iota(jnp.int32, (ncmp_pad, 1), 0) * CMP_STRIDE + (CMP_LEN - 1)
    cvis = rep(n_end <= tpos)
    m = jnp.max(jnp.where(cvis, sc, NEG), axis=0, keepdims=True)
    p = jnp.exp(jnp.where(cvis, sc - m, NEG))
    p = p / jnp.maximum(jnp.sum(p, axis=0, keepdims=True), 1e-30)
    o_cmp = jnp.dot(vc_ref[...].T.astype(BF16), p.astype(BF16), preferred_element_type=F32)
    psum = p[:, 0:tq] + p[:, tq:2 * tq] + p[:, 2 * tq:3 * tq]
    cn = lax.broadcasted_iota(jnp.int32, (n_sel, ncmp_pad), 1) * CMP_STRIDE
    sb = lax.broadcasted_iota(jnp.int32, (n_sel, ncmp_pad), 0) * SLC_BLOCK
    overlap_t = jnp.where((cn < sb + SLC_BLOCK) & (cn + CMP_LEN > sb), 1.0, 0.0).astype(BF16)
    hi = psum.astype(BF16)
    r1 = psum - hi.astype(F32)
    mid = r1.astype(BF16)
    lo = (r1 - mid.astype(F32)).astype(BF16)
    imp = (jnp.dot(overlap_t, hi, preferred_element_type=F32)
           + jnp.dot(overlap_t, mid, preferred_element_type=F32)
           + jnp.dot(overlap_t, lo, preferred_element_type=F32))

    blk = lax.broadcasted_iota(jnp.int32, (n_sel, 1), 0)
    cur = tpos // SLC_BLOCK
    forced = (blk == 0) | (blk == cur) | (blk == cur - 1)
    valid = blk * SLC_BLOCK <= tpos
    score = jnp.where(valid, imp + jnp.where(forced, FORCE_BONUS, 0.0), -jnp.inf)
    rank = jnp.zeros((n_sel, tq), F32)
    for i in range(n_sel):
        ci = score[i:i + 1, :]
        rank += jnp.where(blk > i, jnp.where(ci >= score, 1.0, 0.0), jnp.where(ci > score, 1.0, 0.0))
    sel_mask = jnp.where(rank < float(min(SLC_TOPK, n_sel)), 0.0, NEG)
    for r in range(n_sel):
        selb_scr[r * SLC_BLOCK:(r + 1) * SLC_BLOCK, :] = jnp.broadcast_to(sel_mask[r:r + 1, :], (SLC_BLOCK, tq))

    kio = lax.broadcasted_iota(jnp.int32, (tk, 1), 0)

    def sel_bias(kb):
        return selb_scr[pl.ds(pl.multiple_of(kb * tk, tk), tk), :]

    _, l_s = _attend_range(qall, ksb_scr, vst_scr, 0, (qi + 1) * n_diag, lambda b: rep(sel_bias(b)),
                           lambda b: rep(jnp.where(b * tk + kio <= tpos, sel_bias(b), NEG)), n_diag,
                           s_scr, p_scr, accs_scr, tk)
    o_slc = accs_scr[...] / jnp.maximum(l_s, 1e-30)

    def win_bias(b):
        dist = tpos - (b * tk + kio)
        return rep(jnp.where((dist >= 0) & (dist < WINDOW), 0.0, NEG))

    m_w, l_w = _attend_range(qall, kwb_scr, vwt_scr, jnp.maximum(qi * n_diag - WINDOW // tk, 0),
                             (qi + 1) * n_diag, win_bias, win_bias, 1, sw_scr, pw_scr, accw_scr, tk)
    n_pad = rep(jnp.maximum(WINDOW - 1 - tpos, 0)).astype(F32)
    m_p = jnp.where(n_pad > 0.0, jnp.maximum(m_w, 0.0), m_w)
    a_p = jnp.exp(m_w - m_p)
    l_p = jnp.where(n_pad > 0.0, n_pad * jnp.exp(-jnp.maximum(m_p, 0.0)), 0.0)
    o_win = (accw_scr[...] * a_p) / jnp.maximum(l_w * a_p + l_p, 1e-30)

    gate = jax.nn.sigmoid(gl_ref[...].T)
    for j in range(nj):
        def grow(br):
            c0 = 3 * j + br
            c1 = 3 * (nj + j) + br
            return jnp.where(grp == 0, gate[c0:c0 + 1, :], gate[c1:c1 + 1, :])
        lanes = slice(j * tq, (j + 1) * tq)
        y_t = grow(0) * o_cmp[:, lanes] + grow(1) * o_slc[:, lanes] + grow(2) * o_win[:, lanes]
        o_ref[:, j * NSA_DIM:(j + 1) * NSA_DIM] = y_t.T.astype(BF16)


def _nsa(proj, gate_logits, cmp, batch, seq, tq=256):
    nq = seq // tq
    g = NSA_KV_HEADS
    gw = NSA_GROUP * NSA_DIM
    ncmp_pad = seq // CMP_STRIDE

    def kv(cb):
        return pl.BlockSpec((seq, LANE), lambda b, j, i: (b, cb + j))

    def cmp_spec(which):
        return pl.BlockSpec((None, None, None, ncmp_pad, NSA_DIM), lambda b, j, i: (b, which, j, 0, 0))

    kb_scr = pltpu.VMEM((seq, LANE), BF16)
    vt_scr = pltpu.VMEM((NSA_DIM, seq), BF16)
    acc_scr = pltpu.VMEM((NSA_DIM, NSA_GROUP * tq), F32)
    s_scr = pltpu.VMEM((ATT_TK, NSA_GROUP * tq), F32)
    p_scr = pltpu.VMEM((ATT_TK, NSA_GROUP * tq), BF16)
    return pl.pallas_call(
        functools.partial(_nsa_kernel, tq=tq, seq=seq),
        grid=(batch, g, nq),
        in_specs=[pl.BlockSpec((tq, gw), lambda b, j, i: (b * nq + i, CB_CQ * LANE // gw + j)),
                  cmp_spec(0), cmp_spec(1),
                  kv(CB_KS), kv(CB_VS), kv(CB_KW), kv(CB_VW),
                  pl.BlockSpec((tq, LANE), lambda b, j, i: (b * nq + i, 0))],
        out_specs=pl.BlockSpec((tq, gw), lambda b, j, i: (b * nq + i, j)),
        out_shape=jax.ShapeDtypeStruct((batch * seq, D_C), BF16),
        scratch_shapes=[kb_scr, vt_scr, kb_scr, vt_scr, pltpu.VMEM((seq, tq), F32),
                        s_scr, p_scr, s_scr, p_scr, acc_scr, acc_scr],
        compiler_params=_cparams(("parallel", "parallel", "arbitrary")),
        name="nsa",
    )(proj, cmp, cmp, proj, proj, proj, proj, gate_logits)


def kernel(x, attn_norm, w_in, da_lam_q1, da_lam_k1, da_lam_q2, da_lam_k2, da_norm, hg_gamma, hg_norm,
           nsa_pe_k, nsa_pe_v, nsa_ck_w1, nsa_ck_w2, nsa_cv_w1, nsa_cv_w2, w_out, ffn_norm,
           w_gate, w_up, w_down, final_norm):
    batch, seq, d = x.shape
    depth = w_in.shape[0]
    assert (CB_CQ * LANE) % (NSA_GROUP * NSA_DIM) == 0 and D_IN_MAIN + NSA_HEADS * 3 == D_IN
    xf = x.reshape(batch * seq, d)
    w_in_tail = jnp.pad(w_in[:, :, D_IN_MAIN:], ((0, 0), (0, 0), (0, LANE - (D_IN - D_IN_MAIN))))
    for l in range(depth):
        proj, gate_logits = _norm_matmul(xf, attn_norm[l], w_in, w_in_tail, l)
        ya = _diff_attn(proj, da_lam_q1[l], da_lam_k1[l], da_lam_q2[l], da_lam_k2[l], da_norm[l],
                        l, batch, seq)
        yb = _hgrn(proj, hg_gamma, hg_norm[l], l, batch, seq)
        cmp = _compress(proj, nsa_pe_k, nsa_pe_v, nsa_ck_w1, nsa_cv_w1, nsa_ck_w2, nsa_cv_w2, l, batch, seq)
        yc = _nsa(proj, gate_logits, cmp, batch, seq)
        xf = _out_proj(ya, yb, yc, w_out, xf, l)
        u = _ffn_up(xf, ffn_norm[l], w_gate, w_up, l)
        xf = _matmul_res(u, w_down, xf, l)
    return _final_norm(xf, final_norm).reshape(batch, seq, d)
```

```python
import functools
import math

import jax
import jax.numpy as jnp
import numpy as np
from jax import lax
from jax.experimental import pallas as pl
from jax.experimental.pallas import tpu as pltpu

F32 = jnp.float32
BF16 = jnp.bfloat16

D_MODEL = 2048
DA_HEADS = 4
DA_QK_DIM = 64
DA_V_DIM = 128
HG_HEADS = 6
HG_DIM = 128
NSA_HEADS = 6
NSA_KV_HEADS = 2
NSA_GROUP = NSA_HEADS // NSA_KV_HEADS
NSA_DIM = 128
CMP_LEN = 32
CMP_STRIDE = 16
CMP_HIDDEN = 256
SLC_BLOCK = 64
SLC_TOPK = 16
WINDOW = 512
FORCE_BONUS = 1.0e4
D_A = DA_HEADS * DA_V_DIM
D_B = HG_HEADS * HG_DIM
D_C = NSA_HEADS * NSA_DIM
D_FF = ((8 * D_MODEL // 3 + 255) // 256) * 256
D_IN = 3 * D_A + 4 * D_B + D_C + 6 * NSA_KV_HEADS * NSA_DIM + NSA_HEADS * 3

LANE = 128
CB_AQ, CB_AK, CB_AV = 0, 4, 8
CB_BF, CB_BQ, CB_BI, CB_BG = 12, 18, 24, 30
CB_CQ = 36
CB_KC, CB_VC, CB_KS, CB_VS, CB_KW, CB_VW = 42, 44, 46, 48, 50, 52
D_IN_MAIN = 54 * LANE
D_IN_PAD = D_IN_MAIN

EPS = 1e-6
NEG = -1e30
LOG2E = 1.4426950408889634
VMEM_LIMIT = 56 * 1024 * 1024

_NT = (((1,), (1,)), ((), ()))


def _cparams(sem):
    return pltpu.CompilerParams(dimension_semantics=sem, vmem_limit_bytes=VMEM_LIMIT)


def _rms_rows(x, gain):
    return x * lax.rsqrt(jnp.mean(x * x, axis=-1, keepdims=True) + EPS) * gain


def _norm_matmul_kernel(x_ref, g_ref, w_ref, wt_ref, o_ref, ot_ref, h_scr):
    @pl.when(pl.program_id(1) == 0)
    def _():
        h = _rms_rows(x_ref[...], g_ref[...]).astype(BF16)
        h_scr[...] = h
        ot_ref[...] = jnp.dot(h, wt_ref[...].astype(BF16), preferred_element_type=F32)

    o_ref[...] = jnp.dot(h_scr[...], w_ref[...], preferred_element_type=F32)


def _norm_matmul(x, gain, w, w_tail, layer, tm=1024, tn=768):
    m, k = x.shape
    nt = w_tail.shape[2]
    return pl.pallas_call(
        _norm_matmul_kernel,
        grid=(m // tm, D_IN_MAIN // tn),
        in_specs=[pl.BlockSpec((tm, k), lambda i, j: (i, 0)),
                  pl.BlockSpec((1, k), lambda i, j: (0, 0)),
                  pl.BlockSpec((None, k, tn), lambda i, j: (layer, 0, j)),
                  pl.BlockSpec((None, k, nt), lambda i, j: (layer, 0, 0))],
        out_specs=[pl.BlockSpec((tm, tn), lambda i, j: (i, j)),
                   pl.BlockSpec((tm, nt), lambda i, j: (i, 0))],
        out_shape=[jax.ShapeDtypeStruct((m, D_IN_MAIN), F32), jax.ShapeDtypeStruct((m, nt), F32)],
        scratch_shapes=[pltpu.VMEM((tm, k), BF16)],
        compiler_params=_cparams(("parallel", "arbitrary")),
        name="norm_in_proj",
    )(x, gain.reshape(1, k), w, w_tail)


def _ffn_up_kernel(x_ref, g_ref, wg_ref, wu_ref, o_ref, h_scr):
    @pl.when(pl.program_id(1) == 0)
    def _():
        h_scr[...] = _rms_rows(x_ref[...], g_ref[...]).astype(BF16)

    h = h_scr[...]
    a = jnp.dot(h, wg_ref[...].astype(BF16), preferred_element_type=F32)
    b = jnp.dot(h, wu_ref[...].astype(BF16), preferred_element_type=F32)
    o_ref[...] = (a * jax.nn.sigmoid(a) * b).astype(BF16)


def _ffn_up(x, gain, wg, wu, layer, tm=1024, tn=512):
    m, k = x.shape
    n = wg.shape[2]
    return pl.pallas_call(
        _ffn_up_kernel,
        grid=(m // tm, n // tn),
        in_specs=[pl.BlockSpec((tm, k), lambda i, j: (i, 0)),
                  pl.BlockSpec((1, k), lambda i, j: (0, 0)),
                  pl.BlockSpec((None, k, tn), lambda i, j: (layer, 0, j)),
                  pl.BlockSpec((None, k, tn), lambda i, j: (layer, 0, j))],
        out_specs=pl.BlockSpec((tm, tn), lambda i, j: (i, j)),
        out_shape=jax.ShapeDtypeStruct((m, n), BF16),
        scratch_shapes=[pltpu.VMEM((tm, k), BF16)],
        compiler_params=_cparams(("parallel", "arbitrary")),
        name="ffn_up",
    )(x, gain.reshape(1, k), wg, wu)


def _matmul_res_kernel(a_ref, w_ref, r_ref, o_ref):
    o_ref[...] = r_ref[...] + jnp.dot(a_ref[...], w_ref[...].astype(BF16), preferred_element_type=F32)


def _matmul_res(a, w, res, layer, tm=1024, tn=256):
    m, k = a.shape
    n = w.shape[2]
    return pl.pallas_call(
        _matmul_res_kernel,
        grid=(m // tm, n // tn),
        in_specs=[pl.BlockSpec((tm, k), lambda i, j: (i, 0)),
                  pl.BlockSpec((None, k, tn), lambda i, j: (layer, 0, j)),
                  pl.BlockSpec((tm, tn), lambda i, j: (i, j))],
        out_specs=pl.BlockSpec((tm, tn), lambda i, j: (i, j)),
        out_shape=jax.ShapeDtypeStruct((m, n), F32),
        compiler_params=_cparams(("parallel", "arbitrary")),
        name="ffn_down",
    )(a, w, res)


def _out_proj_kernel(ya_ref, yb_ref, yc_ref, w_ref, r_ref, o_ref):
    acc = jnp.dot(ya_ref[...], w_ref[0:D_A, :].astype(BF16), preferred_element_type=F32)
    acc += jnp.dot(yb_ref[...], w_ref[D_A:D_A + D_B, :].astype(BF16), preferred_element_type=F32)
    acc += jnp.dot(yc_ref[...], w_ref[D_A + D_B:, :].astype(BF16), preferred_element_type=F32)
    o_ref[...] = r_ref[...] + acc


def _out_proj(ya, yb, yc, w, res, layer, tm=1024, tn=512):
    m = ya.shape[0]
    _, k, n = w.shape
    return pl.pallas_call(
        _out_proj_kernel,
        grid=(m // tm, n // tn),
        in_specs=[pl.BlockSpec((tm, D_A), lambda i, j: (i, 0)),
                  pl.BlockSpec((tm, D_B), lambda i, j: (i, 0)),
                  pl.BlockSpec((tm, D_C), lambda i, j: (i, 0)),
                  pl.BlockSpec((None, k, tn), lambda i, j: (layer, 0, j)),
                  pl.BlockSpec((tm, tn), lambda i, j: (i, j))],
        out_specs=pl.BlockSpec((tm, tn), lambda i, j: (i, j)),
        out_shape=jax.ShapeDtypeStruct((m, n), F32),
        compiler_params=_cparams(("parallel", "arbitrary")),
        name="out_proj",
    )(ya, yb, yc, w, res)


def _final_norm_kernel(x_ref, g_ref, o_ref):
    o_ref[...] = _rms_rows(x_ref[...], g_ref[...])


def _final_norm(x, gain, tm=512):
    m, k = x.shape
    return pl.pallas_call(
        _final_norm_kernel,
        grid=(m // tm,),
        in_specs=[pl.BlockSpec((tm, k), lambda i: (i, 0)),
                  pl.BlockSpec((1, k), lambda i: (0, 0))],
        out_specs=pl.BlockSpec((tm, k), lambda i: (i, 0)),
        out_shape=jax.ShapeDtypeStruct((m, k), F32),
        compiler_params=_cparams(("parallel",)),
        name="final_norm",
    )(x, gain.reshape(1, k))


ATT_TK = 256


def _softmax_block(s, bias, m, l):
    if bias is not None:
        s = s + bias
    m_new = jnp.maximum(m, jnp.max(s, axis=0, keepdims=True))
    alpha = jnp.exp2(m - m_new)
    p = jnp.exp2(s - m_new)
    return p.astype(BF16), alpha, m_new, alpha * l + jnp.sum(p, axis=0, keepdims=True)


def _attend_range(qall, kb_scr, vt_scr, lo, hi, bias_fn, tail_bias_fn, n_tail, s_scr, p_scr, acc_scr, tk):
    rows = qall.shape[0]

    def scores(b):
        k0 = pl.multiple_of(b * tk, tk)
        return lax.dot_general(kb_scr[pl.ds(k0, tk), :], qall, _NT, preferred_element_type=F32)

    def values(b, p):
        k0 = pl.multiple_of(b * tk, tk)
        return jnp.dot(vt_scr[:, pl.ds(k0, tk)], p, preferred_element_type=F32)

    def stage(b, s_cur, p_prev, m, l, alpha_prev, bias, want_next):
        pv_prev = values(jnp.maximum(b - 1, lo), p_prev)
        s_next = scores(b + 1) if want_next else None
        p, alpha, m, l = _softmax_block(s_cur, bias, m, l)
        acc_scr[...] = alpha_prev * acc_scr[...] + pv_prev
        return s_next, p, m, l, alpha

    acc_scr[...] = jnp.zeros_like(acc_scr)
    s_scr[...] = scores(lo)
    p_scr[...] = jnp.zeros(p_scr.shape, BF16)
    n_main = hi - n_tail - lo
    odd = n_main % 2

    def one(i, carry):
        m, l, alpha_prev = carry
        b = lo + i
        s_next, p, m, l, alpha = stage(b, s_scr[...], p_scr[...], m, l, alpha_prev, bias_fn(b), True)
        s_scr[...] = s_next
        p_scr[...] = p
        return m, l, alpha

    def two(i, carry):
        m, l, alpha_prev = carry
        b = lo + odd + 2 * i
        s_mid, p_mid, m, l, alpha = stage(b, s_scr[...], p_scr[...], m, l, alpha_prev, bias_fn(b), True)
        s_next, p, m, l, alpha = stage(b + 1, s_mid, p_mid, m, l, alpha, bias_fn(b + 1), True)
        s_scr[...] = s_next
        p_scr[...] = p
        return m, l, alpha

    carry = (jnp.full((1, rows), NEG, F32), jnp.zeros((1, rows), F32), jnp.ones((1, rows), F32))
    carry = lax.fori_loop(0, odd, one, carry)
    m, l, alpha_prev = lax.fori_loop(0, n_main // 2, two, carry)
    s_cur = s_scr[...]
    p_prev = p_scr[...]
    for t in range(n_tail):
        b = hi - n_tail + t
        s_cur, p_prev, m, l, alpha_prev = stage(b, s_cur, p_prev, m, l, alpha_prev, tail_bias_fn(b),
                                                t + 1 < n_tail)
    acc_scr[...] = alpha_prev * acc_scr[...] + values(hi - 1, p_prev)
    return m, l


def _stage_kv(k_ref, v_ref, kb_scr, vt_scr, seq, tk):
    for c in range(seq // tk):
        rows = slice(c * tk, (c + 1) * tk)
        kb_scr[rows, :] = k_ref[rows, :].astype(BF16)
        vt_scr[:, rows] = v_ref[rows, :].T.astype(BF16)


def _diff_attn_kernel(q_ref, k_ref, v_ref, lq1_ref, lk1_ref, lq2_ref, lk2_ref, gn_ref, o_ref,
                      kb_scr, vt_scr, s_scr, p_scr, acc_scr, *, lam_init, tq, seq):
    h = pl.program_id(1)
    i = pl.program_id(2)
    tk = ATT_TK
    n_diag = tq // tk

    @pl.when(i == 0)
    def _():
        _stage_kv(k_ref, v_ref, kb_scr, vt_scr, seq, tk)

    q = q_ref[...] * (DA_QK_DIM ** -0.5 * LOG2E)
    lane = lax.broadcasted_iota(jnp.int32, q.shape, 1)
    qbd = jnp.concatenate([jnp.where(lane < DA_QK_DIM, q, 0.0), jnp.where(lane >= DA_QK_DIM, q, 0.0)],
                          axis=0).astype(BF16)
    tpos = i * tq + lax.broadcasted_iota(jnp.int32, (1, tq), 1)
    kio = lax.broadcasted_iota(jnp.int32, (tk, 1), 0)

    def causal(b):
        keep = jnp.where(b * tk + kio <= tpos, 0.0, NEG)
        return jnp.concatenate([keep, keep], axis=1)

    _, l = _attend_range(qbd, kb_scr, vt_scr, 0, (i + 1) * n_diag, lambda b: None, causal, n_diag,
                         s_scr, p_scr, acc_scr, tk)
    o_t = acc_scr[...] / jnp.maximum(l, 1e-30)
    lam = (jnp.exp(jnp.sum(lq1_ref[...] * lk1_ref[...], axis=-1, keepdims=True))
           - jnp.exp(jnp.sum(lq2_ref[...] * lk2_ref[...], axis=-1, keepdims=True)) + lam_init)
    o = (o_t[:, :tq] - lam * o_t[:, tq:]).T
    y = _rms_rows(o, gn_ref[pl.ds(h, 1), :]) * (1.0 - lam_init)
    o_ref[...] = y.astype(BF16)


def _diff_attn(proj, lq1, lk1, lq2, lk2, gn, layer, batch, seq, tq=256):
    nq = seq // tq
    lam_init = 0.8 - 0.6 * math.exp(-0.3 * layer)
    vec = pl.BlockSpec((1, DA_QK_DIM), lambda b, h, i: (0, 0))
    return pl.pallas_call(
        functools.partial(_diff_attn_kernel, lam_init=lam_init, tq=tq, seq=seq),
        grid=(batch, DA_HEADS, nq),
        in_specs=[pl.BlockSpec((tq, LANE), lambda b, h, i: (b * nq + i, CB_AQ + h)),
                  pl.BlockSpec((seq, LANE), lambda b, h, i: (b, CB_AK + h)),
                  pl.BlockSpec((seq, LANE), lambda b, h, i: (b, CB_AV + h)),
                  vec, vec, vec, vec,
                  pl.BlockSpec((DA_HEADS, DA_V_DIM), lambda b, h, i: (0, 0))],
        out_specs=pl.BlockSpec((tq, LANE), lambda b, h, i: (b * nq + i, h)),
        out_shape=jax.ShapeDtypeStruct((batch * seq, D_A), BF16),
        scratch_shapes=[pltpu.VMEM((seq, LANE), BF16), pltpu.VMEM((DA_V_DIM, seq), BF16),
                        pltpu.VMEM((ATT_TK, 2 * tq), F32), pltpu.VMEM((ATT_TK, 2 * tq), BF16),
                        pltpu.VMEM((DA_V_DIM, 2 * tq), F32)],
        compiler_params=_cparams(("parallel", "parallel", "arbitrary")),
        name="diff_attn",
    )(proj, proj, proj, lq1.reshape(1, -1), lk1.reshape(1, -1), lq2.reshape(1, -1), lk2.reshape(1, -1), gn)


HG_LEVELS = (64, 32, 16, 8, 4, 2, 1)
HG_SUB = 2 * HG_LEVELS[0]


def _hgrn_consts():
    tb = HG_SUB
    t = np.arange(tb)[:, None]
    u = np.arange(tb)[None, :]
    sums, owns = [], []
    for c in HG_LEVELS:
        mid = (t // (2 * c)) * (2 * c) + c
        second = (t % (2 * c)) >= c
        sums.append(np.where(second, (u >= mid) & (u <= t), (u > t) & (u < mid)))
        owns.append(((t // (2 * c)) == (u // (2 * c))) & second & ((u % (2 * c)) < c))
    sums.append(u <= t)
    return (jnp.asarray(np.concatenate(sums, 0).astype(np.float32), BF16),
            jnp.asarray(np.stack(owns).astype(np.float32)))


def _hgrn_kernel(f_ref, q_ref, i_ref, g_ref, gam_ref, gn_ref, sums_ref, owns_ref, o_ref, st_scr,
                 *, layer, ts):
    h = pl.program_id(1)
    tb = HG_SUB
    nlev = len(HG_LEVELS)

    @pl.when(pl.program_id(2) == 0)
    def _():
        st_scr[...] = jnp.zeros_like(st_scr)

    if layer > 0:
        gam = gam_ref[...]
        e = jnp.exp(gam - jnp.max(gam, axis=0, keepdims=True))
        lb = jnp.sum(e[1:layer + 1], axis=0, keepdims=True) / jnp.sum(e, axis=0, keepdims=True)
        log_lb = jnp.log(lb)
        log_1mlb = jnp.log(1.0 - lb)
    gain = gn_ref[pl.ds(h, 1), :]
    st = st_scr[...]

    for sb in range(ts // tb):
        rows = slice(sb * tb, (sb + 1) * tb)
        z = f_ref[rows, :]
        soft = jnp.log(1.0 + jnp.exp(-jnp.abs(z)))
        logsig = jnp.minimum(z, 0.0) - soft
        logsig_neg = jnp.minimum(-z, 0.0) - soft
        if layer == 0:
            logf = logsig
            key = jnp.exp(logsig_neg)
        else:
            b = log_1mlb + logsig
            logf = jnp.maximum(log_lb, b) + jnp.log(1.0 + jnp.exp(-jnp.abs(log_lb - b)))
            key = (1.0 - lb) * jnp.exp(logsig_neg)
        logf2 = logf * LOG2E
        hi = logf2.astype(BF16)
        mid = (logf2 - hi.astype(F32)).astype(BF16)
        parts = jnp.dot(sums_ref[...], jnp.concatenate([hi, mid], axis=1), preferred_element_type=F32)

        def expo(i):
            blk = parts[i * tb:(i + 1) * tb]
            return blk[:, 0:HG_DIM] + blk[:, HG_DIM:2 * HG_DIM]

        q = q_ref[rows, :]
        v = i_ref[rows, :]
        vb = v.astype(BF16)
        scores = None
        for i in range(nlev):
            w = jnp.exp2(expo(i))
            a = lax.dot_general((q * w).astype(BF16), (key * w).astype(BF16), _NT, preferred_element_type=F32)
            a = a * owns_ref[i]
            scores = a if scores is None else scores + a
        cum = expo(nlev)
        o = jnp.dot(scores.astype(BF16), vb, preferred_element_type=F32)
        o = o + jnp.sum(q * key, axis=-1, keepdims=True) * v
        o = o + lax.dot_general((q * jnp.exp2(cum)).astype(BF16), st.astype(BF16), _NT, preferred_element_type=F32)
        gt = g_ref[rows, :]
        o_ref[rows, :] = (_rms_rows(o, gain) * (gt * jax.nn.sigmoid(gt))).astype(BF16)
        last = cum[tb - 1:tb, :]
        ke = (key * jnp.exp2(last - cum)).astype(BF16)
        st = st * jnp.exp2(last) + jnp.dot(v.T.astype(BF16), ke, preferred_element_type=F32)

    st_scr[...] = st


def _hgrn(proj, gamma, gn, layer, batch, seq, ts=512):
    nt = seq // ts
    depth = gamma.shape[0]
    sums, owns = _hgrn_consts()

    def blk(cb):
        return pl.BlockSpec((ts, LANE), lambda b, h, t: (b * nt + t, cb + h))

    return pl.pallas_call(
        functools.partial(_hgrn_kernel, layer=layer, ts=ts),
        grid=(batch, HG_HEADS, nt),
        in_specs=[blk(CB_BF), blk(CB_BQ), blk(CB_BI), blk(CB_BG),
                  pl.BlockSpec((depth, LANE), lambda b, h, t: (0, h)),
                  pl.BlockSpec((HG_HEADS, HG_DIM), lambda b, h, t: (0, 0)),
                  pl.BlockSpec(sums.shape, lambda b, h, t: (0, 0)),
                  pl.BlockSpec(owns.shape, lambda b, h, t: (0, 0, 0))],
        out_specs=pl.BlockSpec((ts, LANE), lambda b, h, t: (b * nt + t, h)),
        out_shape=jax.ShapeDtypeStruct((batch * seq, D_B), BF16),
        scratch_shapes=[pltpu.VMEM((HG_DIM, HG_DIM), F32)],
        compiler_params=_cparams(("parallel", "parallel", "arbitrary")),
        name="hgrn2",
    )(proj, proj, proj, proj, gamma, gn, sums, owns)


def _gelu_tanh(x):
    return 0.5 * x * (1.0 + jnp.tanh(math.sqrt(2.0 / math.pi) * (x + 0.044715 * (x * x * x))))


def _compress_kernel(xk_ref, xv_ref, pek_ref, pev_ref, w1k_ref, w1v_ref, w2k_ref, w2v_ref, o_ref, *, nblk):
    half = CMP_LEN // 2
    for which, (x_ref, pe_ref, w1_ref, w2_ref) in enumerate(((xk_ref, pek_ref, w1k_ref, w2k_ref),
                                                             (xv_ref, pev_ref, w1v_ref, w2v_ref))):
        top = jnp.zeros((nblk, CMP_HIDDEN), F32)
        bot = jnp.zeros((nblk, CMP_HIDDEN), F32)
        for r in range(half):
            xr = x_ref[pl.ds(r, nblk, stride=CMP_STRIDE), :]
            top += jnp.dot((xr + pe_ref[r:r + 1, :]).astype(BF16),
                           w1_ref[r * NSA_DIM:(r + 1) * NSA_DIM, :].astype(BF16), preferred_element_type=F32)
            bot += jnp.dot((xr + pe_ref[half + r:half + r + 1, :]).astype(BF16),
                           w1_ref[(half + r) * NSA_DIM:(half + r + 1) * NSA_DIM, :].astype(BF16),
                           preferred_element_type=F32)
        pre = top + pltpu.roll(bot, nblk - 1, 0)
        o_ref[which] = jnp.dot(_gelu_tanh(pre).astype(BF16), w2_ref[...].astype(BF16),
                               preferred_element_type=F32)


def _compress(proj, pe_k, pe_v, w1_k, w1_v, w2_k, w2_v, layer, batch, seq):
    nblk = seq // CMP_STRIDE
    g = NSA_KV_HEADS

    def per_layer(*shape):
        return pl.BlockSpec((None,) + shape, lambda b, j: (layer,) + (0,) * len(shape))

    return pl.pallas_call(
        functools.partial(_compress_kernel, nblk=nblk),
        grid=(batch, g),
        in_specs=[pl.BlockSpec((seq, LANE), lambda b, j: (b, CB_KC + j)),
                  pl.BlockSpec((seq, LANE), lambda b, j: (b, CB_VC + j)),
                  per_layer(CMP_LEN, NSA_DIM), per_layer(CMP_LEN, NSA_DIM),
                  per_layer(CMP_LEN * NSA_DIM, CMP_HIDDEN), per_layer(CMP_LEN * NSA_DIM, CMP_HIDDEN),
                  per_layer(CMP_HIDDEN, NSA_DIM), per_layer(CMP_HIDDEN, NSA_DIM)],
        out_specs=pl.BlockSpec((None, 2, None, nblk, NSA_DIM), lambda b, j: (b, 0, j, 0, 0)),
        out_shape=jax.ShapeDtypeStruct((batch, 2, g, nblk, NSA_DIM), F32),
        compiler_params=_cparams(("parallel", "parallel")),
        name="nsa_compress",
    )(proj, proj, pe_k, pe_v, w1_k, w1_v, w2_k, w2_v)


def _nsa_kernel(q_ref, kc_ref, vc_ref, ks_ref, vs_ref, kw_ref, vw_ref, gl_ref, o_ref,
                ksb_scr, vst_scr, kwb_scr, vwt_scr, selb_scr, s_scr, p_scr, sw_scr, pw_scr, accs_scr, accw_scr, *, tq, seq):
    grp = pl.program_id(1)
    qi = pl.program_id(2)
    nj = NSA_GROUP
    tk = ATT_TK
    n_diag = tq // tk
    n_sel = seq // SLC_BLOCK
    ncmp_pad = seq // CMP_STRIDE

    @pl.when(qi == 0)
    def _():
        _stage_kv(ks_ref, vs_ref, ksb_scr, vst_scr, seq, tk)
        _stage_kv(kw_ref, vw_ref, kwb_scr, vwt_scr, seq, tk)

    qall = jnp.concatenate([q_ref[:, j * NSA_DIM:(j + 1) * NSA_DIM] for j in range(nj)], axis=0)
    qall = (qall * (NSA_DIM ** -0.5 * LOG2E)).astype(BF16)
    tpos = qi * tq + lax.broadcasted_iota(jnp.int32, (1, tq), 1)

    def rep(x):
        return jnp.concatenate([x] * nj, axis=1)

    sc = lax.dot_general(kc_ref[...].astype(BF16), qall, _NT, preferred_element_type=F32)
    n_end = lax.broadcasted_iota(jnp.int32, (ncmp_pad, 1), 0) * CMP_STRIDE + (CMP_LEN - 1)
    cvis = rep(n_end <= tpos)
    m = jnp.max(jnp.where(cvis, sc, NEG), axis=0, keepdims=True)
    p = jnp.exp2(jnp.where(cvis, sc - m, NEG))
    p = p / jnp.maximum(jnp.sum(p, axis=0, keepdims=True), 1e-30)
    o_cmp = jnp.dot(vc_ref[...].T.astype(BF16), p.astype(BF16), preferred_element_type=F32)
    psum = p[:, 0:tq] + p[:, tq:2 * tq] + p[:, 2 * tq:3 * tq]
    cn = lax.broadcasted_iota(jnp.int32, (n_sel, ncmp_pad), 1) * CMP_STRIDE
    sb = lax.broadcasted_iota(jnp.int32, (n_sel, ncmp_pad), 0) * SLC_BLOCK
    overlap_t = jnp.where((cn < sb + SLC_BLOCK) & (cn + CMP_LEN > sb), 1.0, 0.0).astype(BF16)
    hi = psum.astype(BF16)
    r1 = psum - hi.astype(F32)
    mid = r1.astype(BF16)
    lo = (r1 - mid.astype(F32)).astype(BF16)
    imp = (jnp.dot(overlap_t, hi, preferred_element_type=F32)
           + jnp.dot(overlap_t, mid, preferred_element_type=F32)
           + jnp.dot(overlap_t, lo, preferred_element_type=F32))

    blk = lax.broadcasted_iota(jnp.int32, (n_sel, 1), 0)
    cur = tpos // SLC_BLOCK
    forced = (blk == 0) | (blk == cur) | (blk == cur - 1)
    valid = blk * SLC_BLOCK <= tpos
    score = jnp.where(valid, imp + jnp.where(forced, FORCE_BONUS, 0.0), -jnp.inf)
    rank = jnp.zeros((n_sel, tq), F32)
    for i in range(n_sel):
        ci = score[i:i + 1, :]
        rank += jnp.where(blk > i, jnp.where(ci >= score, 1.0, 0.0), jnp.where(ci > score, 1.0, 0.0))
    sel_mask = jnp.where(rank < float(min(SLC_TOPK, n_sel)), 0.0, NEG)
    for r in range(n_sel):
        selb_scr[r * SLC_BLOCK:(r + 1) * SLC_BLOCK, :] = jnp.broadcast_to(sel_mask[r:r + 1, :], (SLC_BLOCK, tq))

    kio = lax.broadcasted_iota(jnp.int32, (tk, 1), 0)

    def sel_bias(kb):
        return selb_scr[pl.ds(pl.multiple_of(kb * tk, tk), tk), :]

    _, l_s = _attend_range(qall, ksb_scr, vst_scr, 0, (qi + 1) * n_diag, lambda b: rep(sel_bias(b)),
                           lambda b: rep(jnp.where(b * tk + kio <= tpos, sel_bias(b), NEG)), n_diag,
                           s_scr, p_scr, accs_scr, tk)
    o_slc = accs_scr[...] / jnp.maximum(l_s, 1e-30)

    def win_bias(b):
        dist = tpos - (b * tk + kio)
        return rep(jnp.where((dist >= 0) & (dist < WINDOW), 0.0, NEG))

    m_w, l_w = _attend_range(qall, kwb_scr, vwt_scr, jnp.maximum(qi * n_diag - WINDOW // tk, 0),
                             (qi + 1) * n_diag, win_bias, win_bias, 1, sw_scr, pw_scr, accw_scr, tk)
    n_pad = rep(jnp.maximum(WINDOW - 1 - tpos, 0)).astype(F32)
    m_p = jnp.where(n_pad > 0.0, jnp.maximum(m_w, 0.0), m_w)
    a_p = jnp.exp2(m_w - m_p)
    l_p = jnp.where(n_pad > 0.0, n_pad * jnp.exp2(-jnp.maximum(m_p, 0.0)), 0.0)
    o_win = (accw_scr[...] * a_p) / jnp.maximum(l_w * a_p + l_p, 1e-30)

    gate = jax.nn.sigmoid(gl_ref[...].T)
    for j in range(nj):
        def grow(br):
            c0 = 3 * j + br
            c1 = 3 * (nj + j) + br
            return jnp.where(grp == 0, gate[c0:c0 + 1, :], gate[c1:c1 + 1, :])
        lanes = slice(j * tq, (j + 1) * tq)
        y_t = grow(0) * o_cmp[:, lanes] + grow(1) * o_slc[:, lanes] + grow(2) * o_win[:, lanes]
        o_ref[:, j * NSA_DIM:(j + 1) * NSA_DIM] = y_t.T.astype(BF16)


def _nsa(proj, gate_logits, cmp, batch, seq, tq=256):
    nq = seq // tq
    g = NSA_KV_HEADS
    gw = NSA_GROUP * NSA_DIM
    ncmp_pad = seq // CMP_STRIDE

    def kv(cb):
        return pl.BlockSpec((seq, LANE), lambda b, j, i: (b, cb + j))

    def cmp_spec(which):
        return pl.BlockSpec((None, None, None, ncmp_pad, NSA_DIM), lambda b, j, i: (b, which, j, 0, 0))

    kb_scr = pltpu.VMEM((seq, LANE), BF16)
    vt_scr = pltpu.VMEM((NSA_DIM, seq), BF16)
    acc_scr = pltpu.VMEM((NSA_DIM, NSA_GROUP * tq), F32)
    s_scr = pltpu.VMEM((ATT_TK, NSA_GROUP * tq), F32)
    p_scr = pltpu.VMEM((ATT_TK, NSA_GROUP * tq), BF16)
    return pl.pallas_call(
        functools.partial(_nsa_kernel, tq=tq, seq=seq),
        grid=(batch, g, nq),
        in_specs=[pl.BlockSpec((tq, gw), lambda b, j, i: (b * nq + i, CB_CQ * LANE // gw + j)),
                  cmp_spec(0), cmp_spec(1),
                  kv(CB_KS), kv(CB_VS), kv(CB_KW), kv(CB_VW),
                  pl.BlockSpec((tq, LANE), lambda b, j, i: (b * nq + i, 0))],
        out_specs=pl.BlockSpec((tq, gw), lambda b, j, i: (b * nq + i, j)),
        out_shape=jax.ShapeDtypeStruct((batch * seq, D_C), BF16),
        scratch_shapes=[kb_scr, vt_scr, kb_scr, vt_scr, pltpu.VMEM((seq, tq), F32),
                        s_scr, p_scr, s_scr, p_scr, acc_scr, acc_scr],
        compiler_params=_cparams(("parallel", "parallel", "arbitrary")),
        name="nsa",
    )(proj, cmp, cmp, proj, proj, proj, proj, gate_logits)


def kernel(x, attn_norm, w_in, da_lam_q1, da_lam_k1, da_lam_q2, da_lam_k2, da_norm, hg_gamma, hg_norm,
           nsa_pe_k, nsa_pe_v, nsa_ck_w1, nsa_ck_w2, nsa_cv_w1, nsa_cv_w2, w_out, ffn_norm,
           w_gate, w_up, w_down, final_norm):
    batch, seq, d = x.shape
    depth = w_in.shape[0]
    assert (CB_CQ * LANE) % (NSA_GROUP * NSA_DIM) == 0 and D_IN_MAIN + NSA_HEADS * 3 == D_IN
    xf = x.reshape(batch * seq, d)
    w_in_tail = jnp.pad(w_in[:, :, D_IN_MAIN:], ((0, 0), (0, 0), (0, LANE - (D_IN - D_IN_MAIN))))
    w_in_b = w_in.astype(BF16)
    for l in range(depth):
        proj, gate_logits = _norm_matmul(xf, attn_norm[l], w_in_b, w_in_tail, l)
        ya = _diff_attn(proj, da_lam_q1[l], da_lam_k1[l], da_lam_q2[l], da_lam_k2[l], da_norm[l],
                        l, batch, seq)
        yb = _hgrn(proj, hg_gamma, hg_norm[l], l, batch, seq)
        cmp = _compress(proj, nsa_pe_k, nsa_pe_v, nsa_ck_w1, nsa_cv_w1, nsa_ck_w2, nsa_cv_w2, l, batch, seq)
        yc = _nsa(proj, gate_logits, cmp, batch, seq)
        xf = _out_proj(ya, yb, yc, w_out, xf, l)
        u = _ffn_up(xf, ffn_norm[l], w_gate, w_up, l)
        xf = _matmul_res(u, w_down, xf, l)
    return _final_norm(xf, final_norm).reshape(batch, seq, d)
```

```python
import functools
import math

import jax
import jax.numpy as jnp
import numpy as np
from jax import lax
from jax.experimental import pallas as pl
from jax.experimental.pallas import tpu as pltpu

F32 = jnp.float32
BF16 = jnp.bfloat16

D_MODEL = 2048
DA_HEADS = 4
DA_QK_DIM = 64
DA_V_DIM = 128
HG_HEADS = 6
HG_DIM = 128
NSA_HEADS = 6
NSA_KV_HEADS = 2
NSA_GROUP = NSA_HEADS // NSA_KV_HEADS
NSA_DIM = 128
CMP_LEN = 32
CMP_STRIDE = 16
CMP_HIDDEN = 256
SLC_BLOCK = 64
SLC_TOPK = 16
WINDOW = 512
FORCE_BONUS = 1.0e4
D_A = DA_HEADS * DA_V_DIM
D_B = HG_HEADS * HG_DIM
D_C = NSA_HEADS * NSA_DIM
D_FF = ((8 * D_MODEL // 3 + 255) // 256) * 256
D_IN = 3 * D_A + 4 * D_B + D_C + 6 * NSA_KV_HEADS * NSA_DIM + NSA_HEADS * 3

LANE = 128
CB_AQ, CB_AK, CB_AV = 0, 4, 8
CB_BF, CB_BQ, CB_BI, CB_BG = 12, 18, 24, 30
CB_CQ = 36
CB_KC, CB_VC, CB_KS, CB_VS, CB_KW, CB_VW = 42, 44, 46, 48, 50, 52
D_IN_MAIN = 54 * LANE
D_IN_PAD = D_IN_MAIN

EPS = 1e-6
NEG = -1e30
LOG2E = 1.4426950408889634
VMEM_LIMIT = 56 * 1024 * 1024

_NT = (((1,), (1,)), ((), ()))


def _cparams(sem):
    return pltpu.CompilerParams(dimension_semantics=sem, vmem_limit_bytes=VMEM_LIMIT)


def _rms_rows(x, gain):
    return x * lax.rsqrt(jnp.mean(x * x, axis=-1, keepdims=True) + EPS) * gain


W_IN_ROWS = 256
D_IN_ROWS = -(-D_IN // W_IN_ROWS) * W_IN_ROWS


def _w_in_rows_kernel(w_ref, o_ref, *, depth, nchunk):
    rows = o_ref.shape[1]
    n_id = pl.program_id(0) * rows + lax.broadcasted_iota(jnp.int32, (rows, 1), 0)
    keep = n_id < D_IN
    by_chunk = jnp.swapaxes(w_ref[...], 0, 1)
    for c in range(nchunk):
        for l in range(depth):
            o_ref[l, :, c * LANE:(c + 1) * LANE] = jnp.where(keep, by_chunk[c * depth + l], 0.0).astype(BF16)


def _w_in_rows(w_in):
    depth, k, n = w_in.shape
    nchunk = k // LANE
    flat = w_in.reshape(depth, nchunk, LANE, n).transpose(3, 1, 0, 2).reshape(n, nchunk * depth, LANE)
    return pl.pallas_call(
        functools.partial(_w_in_rows_kernel, depth=depth, nchunk=nchunk),
        grid=(D_IN_ROWS // W_IN_ROWS,),
        in_specs=[pl.BlockSpec((W_IN_ROWS, nchunk * depth, LANE), lambda i: (i, 0, 0))],
        out_specs=pl.BlockSpec((depth, W_IN_ROWS, k), lambda i: (0, i, 0)),
        out_shape=jax.ShapeDtypeStruct((depth, D_IN_ROWS, k), BF16),
        compiler_params=_cparams(("parallel",)),
        name="w_in_rows",
    )(flat)


def _norm_matmul_kernel(x_ref, g_ref, w_ref, wt_ref, o_ref, ot_ref, h_scr):
    @pl.when(pl.program_id(1) == 0)
    def _():
        h = _rms_rows(x_ref[...], g_ref[...]).astype(BF16)
        h_scr[...] = h
        ot_ref[...] = lax.dot_general(h, wt_ref[...], _NT, preferred_element_type=F32)

    o_ref[...] = lax.dot_general(h_scr[...], w_ref[...], _NT, preferred_element_type=F32)


def _norm_matmul(x, gain, w_rows, layer, tm=1024, tn=768):
    m, k = x.shape
    return pl.pallas_call(
        _norm_matmul_kernel,
        grid=(m // tm, D_IN_MAIN // tn),
        in_specs=[pl.BlockSpec((tm, k), lambda i, j: (i, 0)),
                  pl.BlockSpec((1, k), lambda i, j: (0, 0)),
                  pl.BlockSpec((None, tn, k), lambda i, j: (layer, j, 0)),
                  pl.BlockSpec((None, LANE, k), lambda i, j: (layer, D_IN_MAIN // LANE, 0))],
        out_specs=[pl.BlockSpec((tm, tn), lambda i, j: (i, j)),
                   pl.BlockSpec((tm, LANE), lambda i, j: (i, 0))],
        out_shape=[jax.ShapeDtypeStruct((m, D_IN_MAIN), F32), jax.ShapeDtypeStruct((m, LANE), F32)],
        scratch_shapes=[pltpu.VMEM((tm, k), BF16)],
        compiler_params=_cparams(("parallel", "arbitrary")),
        name="norm_in_proj",
    )(x, gain.reshape(1, k), w_rows, w_rows)


def _ffn_up_kernel(x_ref, g_ref, wg_ref, wu_ref, o_ref, h_scr):
    @pl.when(pl.program_id(1) == 0)
    def _():
        h_scr[...] = _rms_rows(x_ref[...], g_ref[...]).astype(BF16)

    h = h_scr[...]
    a = jnp.dot(h, wg_ref[...].astype(BF16), preferred_element_type=F32)
    b = jnp.dot(h, wu_ref[...].astype(BF16), preferred_element_type=F32)
    o_ref[...] = (a * jax.nn.sigmoid(a) * b).astype(BF16)


def _ffn_up(x, gain, wg, wu, layer, tm=1024, tn=512):
    m, k = x.shape
    n = wg.shape[2]
    return pl.pallas_call(
        _ffn_up_kernel,
        grid=(m // tm, n // tn),
        in_specs=[pl.BlockSpec((tm, k), lambda i, j: (i, 0)),
                  pl.BlockSpec((1, k), lambda i, j: (0, 0)),
                  pl.BlockSpec((None, k, tn), lambda i, j: (layer, 0, j)),
                  pl.BlockSpec((None, k, tn), lambda i, j: (layer, 0, j))],
        out_specs=pl.BlockSpec((tm, tn), lambda i, j: (i, j)),
        out_shape=jax.ShapeDtypeStruct((m, n), BF16),
        scratch_shapes=[pltpu.VMEM((tm, k), BF16)],
        compiler_params=_cparams(("parallel", "arbitrary")),
        name="ffn_up",
    )(x, gain.reshape(1, k), wg, wu)


def _matmul_res_kernel(a_ref, w_ref, r_ref, o_ref):
    o_ref[...] = r_ref[...] + jnp.dot(a_ref[...], w_ref[...].astype(BF16), preferred_element_type=F32)


def _matmul_res(a, w, res, layer, tm=1024, tn=256):
    m, k = a.shape
    n = w.shape[2]
    return pl.pallas_call(
        _matmul_res_kernel,
        grid=(m // tm, n // tn),
        in_specs=[pl.BlockSpec((tm, k), lambda i, j: (i, 0)),
                  pl.BlockSpec((None, k, tn), lambda i, j: (layer, 0, j)),
                  pl.BlockSpec((tm, tn), lambda i, j: (i, j))],
        out_specs=pl.BlockSpec((tm, tn), lambda i, j: (i, j)),
        out_shape=jax.ShapeDtypeStruct((m, n), F32),
        compiler_params=_cparams(("parallel", "arbitrary")),
        name="ffn_down",
    )(a, w, res)


def _out_proj_kernel(ya_ref, yb_ref, yc_ref, w_ref, r_ref, o_ref):
    acc = jnp.dot(ya_ref[...], w_ref[0:D_A, :].astype(BF16), preferred_element_type=F32)
    acc += jnp.dot(yb_ref[...], w_ref[D_A:D_A + D_B, :].astype(BF16), preferred_element_type=F32)
    acc += jnp.dot(yc_ref[...], w_ref[D_A + D_B:, :].astype(BF16), preferred_element_type=F32)
    o_ref[...] = r_ref[...] + acc


def _out_proj(ya, yb, yc, w, res, layer, tm=1024, tn=512):
    m = ya.shape[0]
    _, k, n = w.shape
    return pl.pallas_call(
        _out_proj_kernel,
        grid=(m // tm, n // tn),
        in_specs=[pl.BlockSpec((tm, D_A), lambda i, j: (i, 0)),
                  pl.BlockSpec((tm, D_B), lambda i, j: (i, 0)),
                  pl.BlockSpec((tm, D_C), lambda i, j: (i, 0)),
                  pl.BlockSpec((None, k, tn), lambda i, j: (layer, 0, j)),
                  pl.BlockSpec((tm, tn), lambda i, j: (i, j))],
        out_specs=pl.BlockSpec((tm, tn), lambda i, j: (i, j)),
        out_shape=jax.ShapeDtypeStruct((m, n), F32),
        compiler_params=_cparams(("parallel", "arbitrary")),
        name="out_proj",
    )(ya, yb, yc, w, res)


def _final_norm_kernel(x_ref, g_ref, o_ref):
    o_ref[...] = _rms_rows(x_ref[...], g_ref[...])


def _final_norm(x, gain, tm=512):
    m, k = x.shape
    return pl.pallas_call(
        _final_norm_kernel,
        grid=(m // tm,),
        in_specs=[pl.BlockSpec((tm, k), lambda i: (i, 0)),
                  pl.BlockSpec((1, k), lambda i: (0, 0))],
        out_specs=pl.BlockSpec((tm, k), lambda i: (i, 0)),
        out_shape=jax.ShapeDtypeStruct((m, k), F32),
        compiler_params=_cparams(("parallel",)),
        name="final_norm",
    )(x, gain.reshape(1, k))


ATT_TK = 256


def _softmax_block(s, bias, m, l):
    if bias is not None:
        s = s + bias
    m_new = jnp.maximum(m, jnp.max(s, axis=0, keepdims=True))
    alpha = jnp.exp2(m - m_new)
    p = jnp.exp2(s - m_new)
    return p.astype(BF16), alpha, m_new, alpha * l + jnp.sum(p, axis=0, keepdims=True)


def _attend_range(qall, kb_scr, vt_scr, lo, hi, bias_fn, tail_bias_fn, n_tail, s_scr, p_scr, acc_scr, tk):
    rows = qall.shape[0]

    def scores(b):
        k0 = pl.multiple_of(b * tk, tk)
        return lax.dot_general(kb_scr[pl.ds(k0, tk), :], qall, _NT, preferred_element_type=F32)

    def values(b, p):
        k0 = pl.multiple_of(b * tk, tk)
        return jnp.dot(vt_scr[:, pl.ds(k0, tk)], p, preferred_element_type=F32)

    def stage(b, s_cur, p_prev, m, l, alpha_prev, bias, want_next):
        pv_prev = values(jnp.maximum(b - 1, lo), p_prev)
        s_next = scores(b + 1) if want_next else None
        p, alpha, m, l = _softmax_block(s_cur, bias, m, l)
        acc_scr[...] = alpha_prev * acc_scr[...] + pv_prev
        return s_next, p, m, l, alpha

    acc_scr[...] = jnp.zeros_like(acc_scr)
    s_scr[...] = scores(lo)
    p_scr[...] = jnp.zeros(p_scr.shape, BF16)
    n_main = hi - n_tail - lo
    odd = n_main % 2

    def one(i, carry):
        m, l, alpha_prev = carry
        b = lo + i
        s_next, p, m, l, alpha = stage(b, s_scr[...], p_scr[...], m, l, alpha_prev, bias_fn(b), True)
        s_scr[...] = s_next
        p_scr[...] = p
        return m, l, alpha

    def two(i, carry):
        m, l, alpha_prev = carry
        b = lo + odd + 2 * i
        s_mid, p_mid, m, l, alpha = stage(b, s_scr[...], p_scr[...], m, l, alpha_prev, bias_fn(b), True)
        s_next, p, m, l, alpha = stage(b + 1, s_mid, p_mid, m, l, alpha, bias_fn(b + 1), True)
        s_scr[...] = s_next
        p_scr[...] = p
        return m, l, alpha

    carry = (jnp.full((1, rows), NEG, F32), jnp.zeros((1, rows), F32), jnp.ones((1, rows), F32))
    carry = lax.fori_loop(0, odd, one, carry)
    m, l, alpha_prev = lax.fori_loop(0, n_main // 2, two, carry)
    s_cur = s_scr[...]
    p_prev = p_scr[...]
    for t in range(n_tail):
        b = hi - n_tail + t
        s_cur, p_prev, m, l, alpha_prev = stage(b, s_cur, p_prev, m, l, alpha_prev, tail_bias_fn(b),
                                                t + 1 < n_tail)
    acc_scr[...] = alpha_prev * acc_scr[...] + values(hi - 1, p_prev)
    return m, l


def _stage_kv(k_ref, v_ref, kb_scr, vt_scr, seq, tk):
    for c in range(seq // tk):
        rows = slice(c * tk, (c + 1) * tk)
        kb_scr[rows, :] = k_ref[rows, :].astype(BF16)
        vt_scr[:, rows] = v_ref[rows, :].T.astype(BF16)


def _diff_attn_kernel(q_ref, k_ref, v_ref, lq1_ref, lk1_ref, lq2_ref, lk2_ref, gn_ref, o_ref,
                      kb_scr, vt_scr, s_scr, p_scr, acc_scr, *, lam_init, tq, seq):
    h = pl.program_id(1)
    i = pl.program_id(2)
    tk = ATT_TK
    n_diag = tq // tk

    @pl.when(i == 0)
    def _():
        _stage_kv(k_ref, v_ref, kb_scr, vt_scr, seq, tk)

    q = q_ref[...] * (DA_QK_DIM ** -0.5 * LOG2E)
    lane = lax.broadcasted_iota(jnp.int32, q.shape, 1)
    qbd = jnp.concatenate([jnp.where(lane < DA_QK_DIM, q, 0.0), jnp.where(lane >= DA_QK_DIM, q, 0.0)],
                          axis=0).astype(BF16)
    tpos = i * tq + lax.broadcasted_iota(jnp.int32, (1, tq), 1)
    kio = lax.broadcasted_iota(jnp.int32, (tk, 1), 0)

    def causal(b):
        keep = jnp.where(b * tk + kio <= tpos, 0.0, NEG)
        return jnp.concatenate([keep, keep], axis=1)

    _, l = _attend_range(qbd, kb_scr, vt_scr, 0, (i + 1) * n_diag, lambda b: None, causal, n_diag,
                         s_scr, p_scr, acc_scr, tk)
    o_t = acc_scr[...] / jnp.maximum(l, 1e-30)
    lam = (jnp.exp(jnp.sum(lq1_ref[...] * lk1_ref[...], axis=-1, keepdims=True))
           - jnp.exp(jnp.sum(lq2_ref[...] * lk2_ref[...], axis=-1, keepdims=True)) + lam_init)
    o = (o_t[:, :tq] - lam * o_t[:, tq:]).T
    y = _rms_rows(o, gn_ref[pl.ds(h, 1), :]) * (1.0 - lam_init)
    o_ref[...] = y.astype(BF16)


def _diff_attn(proj, lq1, lk1, lq2, lk2, gn, layer, batch, seq, tq=256):
    nq = seq // tq
    lam_init = 0.8 - 0.6 * math.exp(-0.3 * layer)
    vec = pl.BlockSpec((1, DA_QK_DIM), lambda b, h, i: (0, 0))
    return pl.pallas_call(
        functools.partial(_diff_attn_kernel, lam_init=lam_init, tq=tq, seq=seq),
        grid=(batch, DA_HEADS, nq),
        in_specs=[pl.BlockSpec((tq, LANE), lambda b, h, i: (b * nq + i, CB_AQ + h)),
                  pl.BlockSpec((seq, LANE), lambda b, h, i: (b, CB_AK + h)),
                  pl.BlockSpec((seq, LANE), lambda b, h, i: (b, CB_AV + h)),
                  vec, vec, vec, vec,
                  pl.BlockSpec((DA_HEADS, DA_V_DIM), lambda b, h, i: (0, 0))],
        out_specs=pl.BlockSpec((tq, LANE), lambda b, h, i: (b * nq + i, h)),
        out_shape=jax.ShapeDtypeStruct((batch * seq, D_A), BF16),
        scratch_shapes=[pltpu.VMEM((seq, LANE), BF16), pltpu.VMEM((DA_V_DIM, seq), BF16),
                        pltpu.VMEM((ATT_TK, 2 * tq), F32), pltpu.VMEM((ATT_TK, 2 * tq), BF16),
                        pltpu.VMEM((DA_V_DIM, 2 * tq), F32)],
        compiler_params=_cparams(("parallel", "parallel", "arbitrary")),
        name="diff_attn",
    )(proj, proj, proj, lq1.reshape(1, -1), lk1.reshape(1, -1), lq2.reshape(1, -1), lk2.reshape(1, -1), gn)


HG_LEVELS = (64, 32, 16, 8, 4, 2, 1)
HG_SUB = 2 * HG_LEVELS[0]


def _hgrn_consts():
    tb = HG_SUB
    t = np.arange(tb)[:, None]
    u = np.arange(tb)[None, :]
    sums, owns = [], []
    for c in HG_LEVELS:
        mid = (t // (2 * c)) * (2 * c) + c
        second = (t % (2 * c)) >= c
        sums.append(np.where(second, (u >= mid) & (u <= t), (u > t) & (u < mid)))
        owns.append(((t // (2 * c)) == (u // (2 * c))) & second & ((u % (2 * c)) < c))
    sums.append(u <= t)
    return (jnp.asarray(np.concatenate(sums, 0).astype(np.float32), BF16),
            jnp.asarray(np.stack(owns).astype(np.float32)))


def _hgrn_kernel(f_ref, q_ref, i_ref, g_ref, gam_ref, gn_ref, sums_ref, owns_ref, o_ref, st_scr,
                 *, layer, ts):
    h = pl.program_id(1)
    tb = HG_SUB
    nlev = len(HG_LEVELS)

    @pl.when(pl.program_id(2) == 0)
    def _():
        st_scr[...] = jnp.zeros_like(st_scr)

    if layer > 0:
        gam = gam_ref[...]
        e = jnp.exp(gam - jnp.max(gam, axis=0, keepdims=True))
        lb = jnp.sum(e[1:layer + 1], axis=0, keepdims=True) / jnp.sum(e, axis=0, keepdims=True)
        log_lb = jnp.log(lb)
        log_1mlb = jnp.log(1.0 - lb)
    gain = gn_ref[pl.ds(h, 1), :]
    st = st_scr[...]

    for sb in range(ts // tb):
        rows = slice(sb * tb, (sb + 1) * tb)
        z = f_ref[rows, :]
        soft = jnp.log(1.0 + jnp.exp(-jnp.abs(z)))
        logsig = jnp.minimum(z, 0.0) - soft
        logsig_neg = jnp.minimum(-z, 0.0) - soft
        if layer == 0:
            logf = logsig
            key = jnp.exp(logsig_neg)
        else:
            b = log_1mlb + logsig
            logf = jnp.maximum(log_lb, b) + jnp.log(1.0 + jnp.exp(-jnp.abs(log_lb - b)))
            key = (1.0 - lb) * jnp.exp(logsig_neg)
        logf2 = logf * LOG2E
        hi = logf2.astype(BF16)
        mid = (logf2 - hi.astype(F32)).astype(BF16)
        parts = jnp.dot(sums_ref[...], jnp.concatenate([hi, mid], axis=1), preferred_element_type=F32)

        def expo(i):
            blk = parts[i * tb:(i + 1) * tb]
            return blk[:, 0:HG_DIM] + blk[:, HG_DIM:2 * HG_DIM]

        q = q_ref[rows, :]
        v = i_ref[rows, :]
        vb = v.astype(BF16)
        scores = None
        for i in range(nlev):
            w = jnp.exp2(expo(i))
            a = lax.dot_general((q * w).astype(BF16), (key * w).astype(BF16), _NT, preferred_element_type=F32)
            a = a * owns_ref[i]
            scores = a if scores is None else scores + a
        cum = expo(nlev)
        o = jnp.dot(scores.astype(BF16), vb, preferred_element_type=F32)
        o = o + jnp.sum(q * key, axis=-1, keepdims=True) * v
        o = o + lax.dot_general((q * jnp.exp2(cum)).astype(BF16), st.astype(BF16), _NT, preferred_element_type=F32)
        gt = g_ref[rows, :]
        o_ref[rows, :] = (_rms_rows(o, gain) * (gt * jax.nn.sigmoid(gt))).astype(BF16)
        last = cum[tb - 1:tb, :]
        ke = (key * jnp.exp2(last - cum)).astype(BF16)
        st = st * jnp.exp2(last) + jnp.dot(v.T.astype(BF16), ke, preferred_element_type=F32)

    st_scr[...] = st


def _hgrn(proj, gamma, gn, layer, batch, seq, ts=512):
    nt = seq // ts
    depth = gamma.shape[0]
    sums, owns = _hgrn_consts()

    def blk(cb):
        return pl.BlockSpec((ts, LANE), lambda b, h, t: (b * nt + t, cb + h))

    return pl.pallas_call(
        functools.partial(_hgrn_kernel, layer=layer, ts=ts),
        grid=(batch, HG_HEADS, nt),
        in_specs=[blk(CB_BF), blk(CB_BQ), blk(CB_BI), blk(CB_BG),
                  pl.BlockSpec((depth, LANE), lambda b, h, t: (0, h)),
                  pl.BlockSpec((HG_HEADS, HG_DIM), lambda b, h, t: (0, 0)),
                  pl.BlockSpec(sums.shape, lambda b, h, t: (0, 0)),
                  pl.BlockSpec(owns.shape, lambda b, h, t: (0, 0, 0))],
        out_specs=pl.BlockSpec((ts, LANE), lambda b, h, t: (b * nt + t, h)),
        out_shape=jax.ShapeDtypeStruct((batch * seq, D_B), BF16),
        scratch_shapes=[pltpu.VMEM((HG_DIM, HG_DIM), F32)],
        compiler_params=_cparams(("parallel", "parallel", "arbitrary")),
        name="hgrn2",
    )(proj, proj, proj, proj, gamma, gn, sums, owns)


def _gelu_tanh(x):
    return 0.5 * x * (1.0 + jnp.tanh(math.sqrt(2.0 / math.pi) * (x + 0.044715 * (x * x * x))))


def _compress_kernel(xk_ref, xv_ref, pek_ref, pev_ref, w1k_ref, w1v_ref, w2k_ref, w2v_ref, o_ref, *, nblk):
    half = CMP_LEN // 2
    for which, (x_ref, pe_ref, w1_ref, w2_ref) in enumerate(((xk_ref, pek_ref, w1k_ref, w2k_ref),
                                                             (xv_ref, pev_ref, w1v_ref, w2v_ref))):
        top = jnp.zeros((nblk, CMP_HIDDEN), F32)
        bot = jnp.zeros((nblk, CMP_HIDDEN), F32)
        for r in range(half):
            xr = x_ref[pl.ds(r, nblk, stride=CMP_STRIDE), :]
            top += jnp.dot((xr + pe_ref[r:r + 1, :]).astype(BF16),
                           w1_ref[r * NSA_DIM:(r + 1) * NSA_DIM, :].astype(BF16), preferred_element_type=F32)
            bot += jnp.dot((xr + pe_ref[half + r:half + r + 1, :]).astype(BF16),
                           w1_ref[(half + r) * NSA_DIM:(half + r + 1) * NSA_DIM, :].astype(BF16),
                           preferred_element_type=F32)
        pre = top + pltpu.roll(bot, nblk - 1, 0)
        o_ref[which] = jnp.dot(_gelu_tanh(pre).astype(BF16), w2_ref[...].astype(BF16),
                               preferred_element_type=F32)


def _compress(proj, pe_k, pe_v, w1_k, w1_v, w2_k, w2_v, layer, batch, seq):
    nblk = seq // CMP_STRIDE
    g = NSA_KV_HEADS

    def per_layer(*shape):
        return pl.BlockSpec((None,) + shape, lambda b, j: (layer,) + (0,) * len(shape))

    return pl.pallas_call(
        functools.partial(_compress_kernel, nblk=nblk),
        grid=(batch, g),
        in_specs=[pl.BlockSpec((seq, LANE), lambda b, j: (b, CB_KC + j)),
                  pl.BlockSpec((seq, LANE), lambda b, j: (b, CB_VC + j)),
                  per_layer(CMP_LEN, NSA_DIM), per_layer(CMP_LEN, NSA_DIM),
                  per_layer(CMP_LEN * NSA_DIM, CMP_HIDDEN), per_layer(CMP_LEN * NSA_DIM, CMP_HIDDEN),
                  per_layer(CMP_HIDDEN, NSA_DIM), per_layer(CMP_HIDDEN, NSA_DIM)],
        out_specs=pl.BlockSpec((None, 2, None, nblk, NSA_DIM), lambda b, j: (b, 0, j, 0, 0)),
        out_shape=jax.ShapeDtypeStruct((batch, 2, g, nblk, NSA_DIM), F32),
        compiler_params=_cparams(("parallel", "parallel")),
        name="nsa_compress",
    )(proj, proj, pe_k, pe_v, w1_k, w1_v, w2_k, w2_v)


def _nsa_kernel(q_ref, kc_ref, vc_ref, ks_ref, vs_ref, kw_ref, vw_ref, gl_ref, o_ref,
                ksb_scr, vst_scr, kwb_scr, vwt_scr, selb_scr, s_scr, p_scr, sw_scr, pw_scr, accs_scr, accw_scr, *, tq, seq):
    grp = pl.program_id(1)
    qi = pl.program_id(2)
    nj = NSA_GROUP
    tk = ATT_TK
    n_diag = tq // tk
    n_sel = seq // SLC_BLOCK
    ncmp_pad = seq // CMP_STRIDE

    @pl.when(qi == 0)
    def _():
        _stage_kv(ks_ref, vs_ref, ksb_scr, vst_scr, seq, tk)
        _stage_kv(kw_ref, vw_ref, kwb_scr, vwt_scr, seq, tk)

    qall = jnp.concatenate([q_ref[:, j * NSA_DIM:(j + 1) * NSA_DIM] for j in range(nj)], axis=0)
    qall = (qall * (NSA_DIM ** -0.5 * LOG2E)).astype(BF16)
    tpos = qi * tq + lax.broadcasted_iota(jnp.int32, (1, tq), 1)

    def rep(x):
        return jnp.concatenate([x] * nj, axis=1)

    sc = lax.dot_general(kc_ref[...].astype(BF16), qall, _NT, preferred_element_type=F32)
    n_end = lax.broadcasted_iota(jnp.int32, (ncmp_pad, 1), 0) * CMP_STRIDE + (CMP_LEN - 1)
    cvis = rep(n_end <= tpos)
    m = jnp.max(jnp.where(cvis, sc, NEG), axis=0, keepdims=True)
    p = jnp.exp2(jnp.where(cvis, sc - m, NEG))
    p = p / jnp.maximum(jnp.sum(p, axis=0, keepdims=True), 1e-30)
    o_cmp = jnp.dot(vc_ref[...].T.astype(BF16), p.astype(BF16), preferred_element_type=F32)
    psum = p[:, 0:tq] + p[:, tq:2 * tq] + p[:, 2 * tq:3 * tq]
    cn = lax.broadcasted_iota(jnp.int32, (n_sel, ncmp_pad), 1) * CMP_STRIDE
    sb = lax.broadcasted_iota(jnp.int32, (n_sel, ncmp_pad), 0) * SLC_BLOCK
    overlap_t = jnp.where((cn < sb + SLC_BLOCK) & (cn + CMP_LEN > sb), 1.0, 0.0).astype(BF16)
    hi = psum.astype(BF16)
    r1 = psum - hi.astype(F32)
    mid = r1.astype(BF16)
    lo = (r1 - mid.astype(F32)).astype(BF16)
    imp = (jnp.dot(overlap_t, hi, preferred_element_type=F32)
           + jnp.dot(overlap_t, mid, preferred_element_type=F32)
           + jnp.dot(overlap_t, lo, preferred_element_type=F32))

    blk = lax.broadcasted_iota(jnp.int32, (n_sel, 1), 0)
    cur = tpos // SLC_BLOCK
    forced = (blk == 0) | (blk == cur) | (blk == cur - 1)
    valid = blk * SLC_BLOCK <= tpos
    score = jnp.where(valid, imp + jnp.where(forced, FORCE_BONUS, 0.0), -jnp.inf)
    rank = jnp.zeros((n_sel, tq), F32)
    for i in range(n_sel):
        ci = score[i:i + 1, :]
        rank += jnp.where(blk > i, jnp.where(ci >= score, 1.0, 0.0), jnp.where(ci > score, 1.0, 0.0))
    sel_mask = jnp.where(rank < float(min(SLC_TOPK, n_sel)), 0.0, NEG)
    for r in range(n_sel):
        selb_scr[r * SLC_BLOCK:(r + 1) * SLC_BLOCK, :] = jnp.broadcast_to(sel_mask[r:r + 1, :], (SLC_BLOCK, tq))

    kio = lax.broadcasted_iota(jnp.int32, (tk, 1), 0)

    def sel_bias(kb):
        return selb_scr[pl.ds(pl.multiple_of(kb * tk, tk), tk), :]

    _, l_s = _attend_range(qall, ksb_scr, vst_scr, 0, (qi + 1) * n_diag, lambda b: rep(sel_bias(b)),
                           lambda b: rep(jnp.where(b * tk + kio <= tpos, sel_bias(b), NEG)), n_diag,
                           s_scr, p_scr, accs_scr, tk)
    o_slc = accs_scr[...] / jnp.maximum(l_s, 1e-30)

    def win_bias(b):
        dist = tpos - (b * tk + kio)
        return rep(jnp.where((dist >= 0) & (dist < WINDOW), 0.0, NEG))

    m_w, l_w = _attend_range(qall, kwb_scr, vwt_scr, jnp.maximum(qi * n_diag - WINDOW // tk, 0),
                             (qi + 1) * n_diag, win_bias, win_bias, 1, sw_scr, pw_scr, accw_scr, tk)
    n_pad = rep(jnp.maximum(WINDOW - 1 - tpos, 0)).astype(F32)
    m_p = jnp.where(n_pad > 0.0, jnp.maximum(m_w, 0.0), m_w)
    a_p = jnp.exp2(m_w - m_p)
    l_p = jnp.where(n_pad > 0.0, n_pad * jnp.exp2(-jnp.maximum(m_p, 0.0)), 0.0)
    o_win = (accw_scr[...] * a_p) / jnp.maximum(l_w * a_p + l_p, 1e-30)

    gate = jax.nn.sigmoid(gl_ref[...].T)
    for j in range(nj):
        def grow(br):
            c0 = 3 * j + br
            c1 = 3 * (nj + j) + br
            return jnp.where(grp == 0, gate[c0:c0 + 1, :], gate[c1:c1 + 1, :])
        lanes = slice(j * tq, (j + 1) * tq)
        y_t = grow(0) * o_cmp[:, lanes] + grow(1) * o_slc[:, lanes] + grow(2) * o_win[:, lanes]
        o_ref[:, j * NSA_DIM:(j + 1) * NSA_DIM] = y_t.T.astype(BF16)


def _nsa(proj, gate_logits, cmp, batch, seq, tq=256):
    nq = seq // tq
    g = NSA_KV_HEADS
    gw = NSA_GROUP * NSA_DIM
    ncmp_pad = seq // CMP_STRIDE

    def kv(cb):
        return pl.BlockSpec((seq, LANE), lambda b, j, i: (b, cb + j))

    def cmp_spec(which):
        return pl.BlockSpec((None, None, None, ncmp_pad, NSA_DIM), lambda b, j, i: (b, which, j, 0, 0))

    kb_scr = pltpu.VMEM((seq, LANE), BF16)
    vt_scr = pltpu.VMEM((NSA_DIM, seq), BF16)
    acc_scr = pltpu.VMEM((NSA_DIM, NSA_GROUP * tq), F32)
    s_scr = pltpu.VMEM((ATT_TK, NSA_GROUP * tq), F32)
    p_scr = pltpu.VMEM((ATT_TK, NSA_GROUP * tq), BF16)
    return pl.pallas_call(
        functools.partial(_nsa_kernel, tq=tq, seq=seq),
        grid=(batch, g, nq),
        in_specs=[pl.BlockSpec((tq, gw), lambda b, j, i: (b * nq + i, CB_CQ * LANE // gw + j)),
                  cmp_spec(0), cmp_spec(1),
                  kv(CB_KS), kv(CB_VS), kv(CB_KW), kv(CB_VW),
                  pl.BlockSpec((tq, LANE), lambda b, j, i: (b * nq + i, 0))],
        out_specs=pl.BlockSpec((tq, gw), lambda b, j, i: (b * nq + i, j)),
        out_shape=jax.ShapeDtypeStruct((batch * seq, D_C), BF16),
        scratch_shapes=[kb_scr, vt_scr, kb_scr, vt_scr, pltpu.VMEM((seq, tq), F32),
                        s_scr, p_scr, s_scr, p_scr, acc_scr, acc_scr],
        compiler_params=_cparams(("parallel", "parallel", "arbitrary")),
        name="nsa",
    )(proj, cmp, cmp, proj, proj, proj, proj, gate_logits)


def kernel(x, attn_norm, w_in, da_lam_q1, da_lam_k1, da_lam_q2, da_lam_k2, da_norm, hg_gamma, hg_norm,
           nsa_pe_k, nsa_pe_v, nsa_ck_w1, nsa_ck_w2, nsa_cv_w1, nsa_cv_w2, w_out, ffn_norm,
           w_gate, w_up, w_down, final_norm):
    batch, seq, d = x.shape
    depth = w_in.shape[0]
    assert (CB_CQ * LANE) % (NSA_GROUP * NSA_DIM) == 0 and D_IN_MAIN + NSA_HEADS * 3 == D_IN
    xf = x.reshape(batch * seq, d)
    w_rows = _w_in_rows(w_in)
    for l in range(depth):
        proj, gate_logits = _norm_matmul(xf, attn_norm[l], w_rows, l)
        ya = _diff_attn(proj, da_lam_q1[l], da_lam_k1[l], da_lam_q2[l], da_lam_k2[l], da_norm[l],
                        l, batch, seq)
        yb = _hgrn(proj, hg_gamma, hg_norm[l], l, batch, seq)
        cmp = _compress(proj, nsa_pe_k, nsa_pe_v, nsa_ck_w1, nsa_cv_w1, nsa_ck_w2, nsa_cv_w2, l, batch, seq)
        yc = _nsa(proj, gate_logits, cmp, batch, seq)
        xf = _out_proj(ya, yb, yc, w_out, xf, l)
        u = _ffn_up(xf, ffn_norm[l], w_gate, w_up, l)
        xf = _matmul_res(u, w_down, xf, l)
    return _final_norm(xf, final_norm).reshape(batch, seq, d)
```

```python
import functools
import math

import jax
import jax.numpy as jnp
import numpy as np
from jax import lax
from jax.experimental import pallas as pl
from jax.experimental.pallas import tpu as pltpu

F32 = jnp.float32
BF16 = jnp.bfloat16

D_MODEL = 2048
DA_HEADS = 4
DA_QK_DIM = 64
DA_V_DIM = 128
HG_HEADS = 6
HG_DIM = 128
NSA_HEADS = 6
NSA_KV_HEADS = 2
NSA_GROUP = NSA_HEADS // NSA_KV_HEADS
NSA_DIM = 128
CMP_LEN = 32
CMP_STRIDE = 16
CMP_HIDDEN = 256
SLC_BLOCK = 64
SLC_TOPK = 16
WINDOW = 512
FORCE_BONUS = 1.0e4
D_A = DA_HEADS * DA_V_DIM
D_B = HG_HEADS * HG_DIM
D_C = NSA_HEADS * NSA_DIM
D_FF = ((8 * D_MODEL // 3 + 255) // 256) * 256
D_IN = 3 * D_A + 4 * D_B + D_C + 6 * NSA_KV_HEADS * NSA_DIM + NSA_HEADS * 3

LANE = 128
CB_AQ, CB_AK, CB_AV = 0, 4, 8
CB_BF, CB_BQ, CB_BI, CB_BG = 12, 18, 24, 30
CB_CQ = 36
CB_KC, CB_VC, CB_KS, CB_VS, CB_KW, CB_VW = 42, 44, 46, 48, 50, 52
D_IN_MAIN = 54 * LANE
D_IN_PAD = D_IN_MAIN

EPS = 1e-6
NEG = -1e30
LOG2E = 1.4426950408889634
VMEM_LIMIT = 56 * 1024 * 1024

_NT = (((1,), (1,)), ((), ()))


def _cparams(sem):
    return pltpu.CompilerParams(dimension_semantics=sem, vmem_limit_bytes=VMEM_LIMIT)


def _rms_rows(x, gain):
    return x * lax.rsqrt(jnp.mean(x * x, axis=-1, keepdims=True) + EPS) * gain


W_IN_ROWS = 256
D_IN_ROWS = -(-D_IN // W_IN_ROWS) * W_IN_ROWS


def _w_in_rows_kernel(w_ref, o_ref, *, depth, nchunk):
    rows = o_ref.shape[1]
    n_id = pl.program_id(0) * rows + lax.broadcasted_iota(jnp.int32, (rows, 1), 0)
    keep = n_id < D_IN
    by_chunk = jnp.swapaxes(w_ref[...], 0, 1)
    for c in range(nchunk):
        for l in range(depth):
            o_ref[l, :, c * LANE:(c + 1) * LANE] = jnp.where(keep, by_chunk[c * depth + l], 0.0).astype(BF16)


def _w_in_rows(w_in):
    depth, k, n = w_in.shape
    nchunk = k // LANE
    flat = w_in.reshape(depth, nchunk, LANE, n).transpose(3, 1, 0, 2).reshape(n, nchunk * depth, LANE)
    return pl.pallas_call(
        functools.partial(_w_in_rows_kernel, depth=depth, nchunk=nchunk),
        grid=(D_IN_ROWS // W_IN_ROWS,),
        in_specs=[pl.BlockSpec((W_IN_ROWS, nchunk * depth, LANE), lambda i: (i, 0, 0))],
        out_specs=pl.BlockSpec((depth, W_IN_ROWS, k), lambda i: (0, i, 0)),
        out_shape=jax.ShapeDtypeStruct((depth, D_IN_ROWS, k), BF16),
        compiler_params=_cparams(("parallel",)),
        name="w_in_rows",
    )(flat)


def _norm_matmul_kernel(x_ref, g_ref, w_ref, wt_ref, o_ref, ot_ref, h_scr):
    @pl.when(pl.program_id(1) == 0)
    def _():
        h = _rms_rows(x_ref[...], g_ref[...]).astype(BF16)
        h_scr[...] = h
        ot_ref[...] = lax.dot_general(h, wt_ref[...], _NT, preferred_element_type=F32)

    o_ref[...] = lax.dot_general(h_scr[...], w_ref[...], _NT, preferred_element_type=F32)


def _norm_matmul(x, gain, w_rows, layer, tm=1024, tn=768):
    m, k = x.shape
    return pl.pallas_call(
        _norm_matmul_kernel,
        grid=(m // tm, D_IN_MAIN // tn),
        in_specs=[pl.BlockSpec((tm, k), lambda i, j: (i, 0)),
                  pl.BlockSpec((1, k), lambda i, j: (0, 0)),
                  pl.BlockSpec((None, tn, k), lambda i, j: (layer, j, 0)),
                  pl.BlockSpec((None, LANE, k), lambda i, j: (layer, D_IN_MAIN // LANE, 0))],
        out_specs=[pl.BlockSpec((tm, tn), lambda i, j: (i, j)),
                   pl.BlockSpec((tm, LANE), lambda i, j: (i, 0))],
        out_shape=[jax.ShapeDtypeStruct((m, D_IN_MAIN), F32), jax.ShapeDtypeStruct((m, LANE), F32)],
        scratch_shapes=[pltpu.VMEM((tm, k), BF16)],
        compiler_params=_cparams(("parallel", "arbitrary")),
        name="norm_in_proj",
    )(x, gain.reshape(1, k), w_rows, w_rows)


def _ffn_up_kernel(x_ref, g_ref, wg_ref, wu_ref, o_ref, h_scr):
    @pl.when(pl.program_id(1) == 0)
    def _():
        h_scr[...] = _rms_rows(x_ref[...], g_ref[...]).astype(BF16)

    h = h_scr[...]
    a = jnp.dot(h, wg_ref[...].astype(BF16), preferred_element_type=F32)
    b = jnp.dot(h, wu_ref[...].astype(BF16), preferred_element_type=F32)
    o_ref[...] = (a * jax.nn.sigmoid(a) * b).astype(BF16)


def _ffn_up(x, gain, wg, wu, layer, tm=1024, tn=512):
    m, k = x.shape
    n = wg.shape[2]
    return pl.pallas_call(
        _ffn_up_kernel,
        grid=(m // tm, n // tn),
        in_specs=[pl.BlockSpec((tm, k), lambda i, j: (i, 0)),
                  pl.BlockSpec((1, k), lambda i, j: (0, 0)),
                  pl.BlockSpec((None, k, tn), lambda i, j: (layer, 0, j)),
                  pl.BlockSpec((None, k, tn), lambda i, j: (layer, 0, j))],
        out_specs=pl.BlockSpec((tm, tn), lambda i, j: (i, j)),
        out_shape=jax.ShapeDtypeStruct((m, n), BF16),
        scratch_shapes=[pltpu.VMEM((tm, k), BF16)],
        compiler_params=_cparams(("parallel", "arbitrary")),
        name="ffn_up",
    )(x, gain.reshape(1, k), wg, wu)


def _matmul_res_kernel(a_ref, w_ref, r_ref, o_ref):
    o_ref[...] = r_ref[...] + jnp.dot(a_ref[...], w_ref[...].astype(BF16), preferred_element_type=F32)


def _matmul_res(a, w, res, layer, tm=1024, tn=256):
    m, k = a.shape
    n = w.shape[2]
    return pl.pallas_call(
        _matmul_res_kernel,
        grid=(m // tm, n // tn),
        in_specs=[pl.BlockSpec((tm, k), lambda i, j: (i, 0)),
                  pl.BlockSpec((None, k, tn), lambda i, j: (layer, 0, j)),
                  pl.BlockSpec((tm, tn), lambda i, j: (i, j))],
        out_specs=pl.BlockSpec((tm, tn), lambda i, j: (i, j)),
        out_shape=jax.ShapeDtypeStruct((m, n), F32),
        compiler_params=_cparams(("parallel", "arbitrary")),
        name="ffn_down",
    )(a, w, res)


def _out_proj_kernel(ya_ref, yb_ref, yc_ref, w_ref, r_ref, o_ref):
    acc = jnp.dot(ya_ref[...], w_ref[0:D_A, :].astype(BF16), preferred_element_type=F32)
    acc += jnp.dot(yb_ref[...], w_ref[D_A:D_A + D_B, :].astype(BF16), preferred_element_type=F32)
    acc += jnp.dot(yc_ref[...], w_ref[D_A + D_B:, :].astype(BF16), preferred_element_type=F32)
    o_ref[...] = r_ref[...] + acc


def _out_proj(ya, yb, yc, w, res, layer, tm=2048, tn=512):
    m = ya.shape[0]
    _, k, n = w.shape
    return pl.pallas_call(
        _out_proj_kernel,
        grid=(m // tm, n // tn),
        in_specs=[pl.BlockSpec((tm, D_A), lambda i, j: (i, 0)),
                  pl.BlockSpec((tm, D_B), lambda i, j: (i, 0)),
                  pl.BlockSpec((tm, D_C), lambda i, j: (i, 0)),
                  pl.BlockSpec((None, k, tn), lambda i, j: (layer, 0, j)),
                  pl.BlockSpec((tm, tn), lambda i, j: (i, j))],
        out_specs=pl.BlockSpec((tm, tn), lambda i, j: (i, j)),
        out_shape=jax.ShapeDtypeStruct((m, n), F32),
        compiler_params=_cparams(("parallel", "arbitrary")),
        name="out_proj",
    )(ya, yb, yc, w, res)


def _final_norm_kernel(x_ref, g_ref, o_ref):
    o_ref[...] = _rms_rows(x_ref[...], g_ref[...])


def _final_norm(x, gain, tm=512):
    m, k = x.shape
    return pl.pallas_call(
        _final_norm_kernel,
        grid=(m // tm,),
        in_specs=[pl.BlockSpec((tm, k), lambda i: (i, 0)),
                  pl.BlockSpec((1, k), lambda i: (0, 0))],
        out_specs=pl.BlockSpec((tm, k), lambda i: (i, 0)),
        out_shape=jax.ShapeDtypeStruct((m, k), F32),
        compiler_params=_cparams(("parallel",)),
        name="final_norm",
    )(x, gain.reshape(1, k))


ATT_TK = 256


def _softmax_block(s, bias, m, l):
    if bias is not None:
        s = s + bias
    m_new = jnp.maximum(m, jnp.max(s, axis=0, keepdims=True))
    alpha = jnp.exp2(m - m_new)
    p = jnp.exp2(s - m_new)
    return p.astype(BF16), alpha, m_new, alpha * l + jnp.sum(p, axis=0, keepdims=True)


def _attend_range(qall, kb_scr, vt_scr, lo, hi, bias_fn, tail_bias_fn, n_tail, s_scr, p_scr, acc_scr, tk):
    rows = qall.shape[0]

    def scores(b):
        k0 = pl.multiple_of(b * tk, tk)
        return lax.dot_general(kb_scr[pl.ds(k0, tk), :], qall, _NT, preferred_element_type=F32)

    def values(b, p):
        k0 = pl.multiple_of(b * tk, tk)
        return jnp.dot(vt_scr[:, pl.ds(k0, tk)], p, preferred_element_type=F32)

    def stage(b, s_cur, p_prev, m, l, alpha_prev, bias, want_next):
        pv_prev = values(jnp.maximum(b - 1, lo), p_prev)
        s_next = scores(b + 1) if want_next else None
        p, alpha, m, l = _softmax_block(s_cur, bias, m, l)
        acc_scr[...] = alpha_prev * acc_scr[...] + pv_prev
        return s_next, p, m, l, alpha

    acc_scr[...] = jnp.zeros_like(acc_scr)
    s_scr[...] = scores(lo)
    p_scr[...] = jnp.zeros(p_scr.shape, BF16)
    n_main = hi - n_tail - lo
    odd = n_main % 2

    def one(i, carry):
        m, l, alpha_prev = carry
        b = lo + i
        s_next, p, m, l, alpha = stage(b, s_scr[...], p_scr[...], m, l, alpha_prev, bias_fn(b), True)
        s_scr[...] = s_next
        p_scr[...] = p
        return m, l, alpha

    def two(i, carry):
        m, l, alpha_prev = carry
        b = lo + odd + 2 * i
        s_mid, p_mid, m, l, alpha = stage(b, s_scr[...], p_scr[...], m, l, alpha_prev, bias_fn(b), True)
        s_next, p, m, l, alpha = stage(b + 1, s_mid, p_mid, m, l, alpha, bias_fn(b + 1), True)
        s_scr[...] = s_next
        p_scr[...] = p
        return m, l, alpha

    carry = (jnp.full((1, rows), NEG, F32), jnp.zeros((1, rows), F32), jnp.ones((1, rows), F32))
    carry = lax.fori_loop(0, odd, one, carry)
    m, l, alpha_prev = lax.fori_loop(0, n_main // 2, two, carry)
    s_cur = s_scr[...]
    p_prev = p_scr[...]
    for t in range(n_tail):
        b = hi - n_tail + t
        s_cur, p_prev, m, l, alpha_prev = stage(b, s_cur, p_prev, m, l, alpha_prev, tail_bias_fn(b),
                                                t + 1 < n_tail)
    acc_scr[...] = alpha_prev * acc_scr[...] + values(hi - 1, p_prev)
    return m, l


def _stage_kv(k_ref, v_ref, kb_scr, vt_scr, seq, tk):
    for c in range(seq // tk):
        rows = slice(c * tk, (c + 1) * tk)
        kb_scr[rows, :] = k_ref[rows, :].astype(BF16)
        vt_scr[:, rows] = v_ref[rows, :].T.astype(BF16)


def _diff_attn_kernel(q_ref, k_ref, v_ref, lq1_ref, lk1_ref, lq2_ref, lk2_ref, gn_ref, o_ref,
                      kb_scr, vt_scr, s_scr, p_scr, acc_scr, *, lam_init, tq, seq):
    h = pl.program_id(1)
    i = pl.program_id(2)
    tk = ATT_TK
    n_diag = tq // tk

    @pl.when(i == 0)
    def _():
        _stage_kv(k_ref, v_ref, kb_scr, vt_scr, seq, tk)

    q = q_ref[...] * (DA_QK_DIM ** -0.5 * LOG2E)
    lane = lax.broadcasted_iota(jnp.int32, q.shape, 1)
    qbd = jnp.concatenate([jnp.where(lane < DA_QK_DIM, q, 0.0), jnp.where(lane >= DA_QK_DIM, q, 0.0)],
                          axis=0).astype(BF16)
    tpos = i * tq + lax.broadcasted_iota(jnp.int32, (1, tq), 1)
    kio = lax.broadcasted_iota(jnp.int32, (tk, 1), 0)

    def causal(b):
        keep = jnp.where(b * tk + kio <= tpos, 0.0, NEG)
        return jnp.concatenate([keep, keep], axis=1)

    _, l = _attend_range(qbd, kb_scr, vt_scr, 0, (i + 1) * n_diag, lambda b: None, causal, n_diag,
                         s_scr, p_scr, acc_scr, tk)
    o_t = acc_scr[...] / jnp.maximum(l, 1e-30)
    lam = (jnp.exp(jnp.sum(lq1_ref[...] * lk1_ref[...], axis=-1, keepdims=True))
           - jnp.exp(jnp.sum(lq2_ref[...] * lk2_ref[...], axis=-1, keepdims=True)) + lam_init)
    o = (o_t[:, :tq] - lam * o_t[:, tq:]).T
    y = _rms_rows(o, gn_ref[pl.ds(h, 1), :]) * (1.0 - lam_init)
    o_ref[...] = y.astype(BF16)


def _diff_attn(proj, lq1, lk1, lq2, lk2, gn, layer, batch, seq, tq=512):
    nq = seq // tq
    lam_init = 0.8 - 0.6 * math.exp(-0.3 * layer)
    vec = pl.BlockSpec((1, DA_QK_DIM), lambda b, h, i: (0, 0))
    return pl.pallas_call(
        functools.partial(_diff_attn_kernel, lam_init=lam_init, tq=tq, seq=seq),
        grid=(batch, DA_HEADS, nq),
        in_specs=[pl.BlockSpec((tq, LANE), lambda b, h, i: (b * nq + i, CB_AQ + h)),
                  pl.BlockSpec((seq, LANE), lambda b, h, i: (b, CB_AK + h)),
                  pl.BlockSpec((seq, LANE), lambda b, h, i: (b, CB_AV + h)),
                  vec, vec, vec, vec,
                  pl.BlockSpec((DA_HEADS, DA_V_DIM), lambda b, h, i: (0, 0))],
        out_specs=pl.BlockSpec((tq, LANE), lambda b, h, i: (b * nq + i, h)),
        out_shape=jax.ShapeDtypeStruct((batch * seq, D_A), BF16),
        scratch_shapes=[pltpu.VMEM((seq, LANE), BF16), pltpu.VMEM((DA_V_DIM, seq), BF16),
                        pltpu.VMEM((ATT_TK, 2 * tq), F32), pltpu.VMEM((ATT_TK, 2 * tq), BF16),
                        pltpu.VMEM((DA_V_DIM, 2 * tq), F32)],
        compiler_params=_cparams(("parallel", "parallel", "arbitrary")),
        name="diff_attn",
    )(proj, proj, proj, lq1.reshape(1, -1), lk1.reshape(1, -1), lq2.reshape(1, -1), lk2.reshape(1, -1), gn)


HG_LEVELS = (64, 32, 16, 8, 4, 2, 1)
HG_SUB = 2 * HG_LEVELS[0]


def _hgrn_consts():
    tb = HG_SUB
    t = np.arange(tb)[:, None]
    u = np.arange(tb)[None, :]
    sums, owns = [], []
    for c in HG_LEVELS:
        mid = (t // (2 * c)) * (2 * c) + c
        second = (t % (2 * c)) >= c
        sums.append(np.where(second, (u >= mid) & (u <= t), (u > t) & (u < mid)))
        owns.append(((t // (2 * c)) == (u // (2 * c))) & second & ((u % (2 * c)) < c))
    sums.append(u <= t)
    return (jnp.asarray(np.concatenate(sums, 0).astype(np.float32), BF16),
            jnp.asarray(np.stack(owns).astype(np.float32)))


def _hgrn_kernel(f_ref, q_ref, i_ref, g_ref, gam_ref, gn_ref, sums_ref, owns_ref, o_ref, st_scr,
                 *, layer, ts):
    h = pl.program_id(1)
    tb = HG_SUB
    nlev = len(HG_LEVELS)

    @pl.when(pl.program_id(2) == 0)
    def _():
        st_scr[...] = jnp.zeros_like(st_scr)

    if layer > 0:
        gam = gam_ref[...]
        e = jnp.exp(gam - jnp.max(gam, axis=0, keepdims=True))
        lb = jnp.sum(e[1:layer + 1], axis=0, keepdims=True) / jnp.sum(e, axis=0, keepdims=True)
        log_lb = jnp.log(lb)
        log_1mlb = jnp.log(1.0 - lb)
    gain = gn_ref[pl.ds(h, 1), :]
    st = st_scr[...]

    for sb in range(ts // tb):
        rows = slice(sb * tb, (sb + 1) * tb)
        z = f_ref[rows, :]
        soft = jnp.log(1.0 + jnp.exp(-jnp.abs(z)))
        logsig = jnp.minimum(z, 0.0) - soft
        logsig_neg = jnp.minimum(-z, 0.0) - soft
        if layer == 0:
            logf = logsig
            key = jnp.exp(logsig_neg)
        else:
            b = log_1mlb + logsig
            logf = jnp.maximum(log_lb, b) + jnp.log(1.0 + jnp.exp(-jnp.abs(log_lb - b)))
            key = (1.0 - lb) * jnp.exp(logsig_neg)
        logf2 = logf * LOG2E
        hi = logf2.astype(BF16)
        mid = (logf2 - hi.astype(F32)).astype(BF16)
        parts = jnp.dot(sums_ref[...], jnp.concatenate([hi, mid], axis=1), preferred_element_type=F32)

        def expo(i):
            blk = parts[i * tb:(i + 1) * tb]
            return blk[:, 0:HG_DIM] + blk[:, HG_DIM:2 * HG_DIM]

        q = q_ref[rows, :]
        v = i_ref[rows, :]
        vb = v.astype(BF16)
        scores = None
        for i in range(nlev):
            w = jnp.exp2(expo(i))
            a = lax.dot_general((q * w).astype(BF16), (key * w).astype(BF16), _NT, preferred_element_type=F32)
            a = a * owns_ref[i]
            scores = a if scores is None else scores + a
        cum = expo(nlev)
        o = jnp.dot(scores.astype(BF16), vb, preferred_element_type=F32)
        o = o + jnp.sum(q * key, axis=-1, keepdims=True) * v
        o = o + lax.dot_general((q * jnp.exp2(cum)).astype(BF16), st.astype(BF16), _NT, preferred_element_type=F32)
        gt = g_ref[rows, :]
        o_ref[rows, :] = (_rms_rows(o, gain) * (gt * jax.nn.sigmoid(gt))).astype(BF16)
        last = cum[tb - 1:tb, :]
        ke = (key * jnp.exp2(last - cum)).astype(BF16)
        st = st * jnp.exp2(last) + jnp.dot(v.T.astype(BF16), ke, preferred_element_type=F32)

    st_scr[...] = st


def _hgrn(proj, gamma, gn, layer, batch, seq, ts=1024):
    nt = seq // ts
    depth = gamma.shape[0]
    sums, owns = _hgrn_consts()

    def blk(cb):
        return pl.BlockSpec((ts, LANE), lambda b, h, t: (b * nt + t, cb + h))

    return pl.pallas_call(
        functools.partial(_hgrn_kernel, layer=layer, ts=ts),
        grid=(batch, HG_HEADS, nt),
        in_specs=[blk(CB_BF), blk(CB_BQ), blk(CB_BI), blk(CB_BG),
                  pl.BlockSpec((depth, LANE), lambda b, h, t: (0, h)),
                  pl.BlockSpec((HG_HEADS, HG_DIM), lambda b, h, t: (0, 0)),
                  pl.BlockSpec(sums.shape, lambda b, h, t: (0, 0)),
                  pl.BlockSpec(owns.shape, lambda b, h, t: (0, 0, 0))],
        out_specs=pl.BlockSpec((ts, LANE), lambda b, h, t: (b * nt + t, h)),
        out_shape=jax.ShapeDtypeStruct((batch * seq, D_B), BF16),
        scratch_shapes=[pltpu.VMEM((HG_DIM, HG_DIM), F32)],
        compiler_params=_cparams(("parallel", "parallel", "arbitrary")),
        name="hgrn2",
    )(proj, proj, proj, proj, gamma, gn, sums, owns)


def _gelu_tanh(x):
    return 0.5 * x * (1.0 + jnp.tanh(math.sqrt(2.0 / math.pi) * (x + 0.044715 * (x * x * x))))


def _compress_kernel(xk_ref, xv_ref, pek_ref, pev_ref, w1k_ref, w1v_ref, w2k_ref, w2v_ref, o_ref, *, nblk):
    half = CMP_LEN // 2
    for which, (x_ref, pe_ref, w1_ref, w2_ref) in enumerate(((xk_ref, pek_ref, w1k_ref, w2k_ref),
                                                             (xv_ref, pev_ref, w1v_ref, w2v_ref))):
        top = jnp.zeros((nblk, CMP_HIDDEN), F32)
        bot = jnp.zeros((nblk, CMP_HIDDEN), F32)
        for r in range(half):
            xr = x_ref[pl.ds(r, nblk, stride=CMP_STRIDE), :]
            top += jnp.dot((xr + pe_ref[r:r + 1, :]).astype(BF16),
                           w1_ref[r * NSA_DIM:(r + 1) * NSA_DIM, :].astype(BF16), preferred_element_type=F32)
            bot += jnp.dot((xr + pe_ref[half + r:half + r + 1, :]).astype(BF16),
                           w1_ref[(half + r) * NSA_DIM:(half + r + 1) * NSA_DIM, :].astype(BF16),
                           preferred_element_type=F32)
        pre = top + pltpu.roll(bot, nblk - 1, 0)
        o_ref[which] = jnp.dot(_gelu_tanh(pre).astype(BF16), w2_ref[...].astype(BF16),
                               preferred_element_type=F32)


def _compress(proj, pe_k, pe_v, w1_k, w1_v, w2_k, w2_v, layer, batch, seq):
    nblk = seq // CMP_STRIDE
    g = NSA_KV_HEADS

    def per_layer(*shape):
        return pl.BlockSpec((None,) + shape, lambda b, j: (layer,) + (0,) * len(shape))

    return pl.pallas_call(
        functools.partial(_compress_kernel, nblk=nblk),
        grid=(batch, g),
        in_specs=[pl.BlockSpec((seq, LANE), lambda b, j: (b, CB_KC + j)),
                  pl.BlockSpec((seq, LANE), lambda b, j: (b, CB_VC + j)),
                  per_layer(CMP_LEN, NSA_DIM), per_layer(CMP_LEN, NSA_DIM),
                  per_layer(CMP_LEN * NSA_DIM, CMP_HIDDEN), per_layer(CMP_LEN * NSA_DIM, CMP_HIDDEN),
                  per_layer(CMP_HIDDEN, NSA_DIM), per_layer(CMP_HIDDEN, NSA_DIM)],
        out_specs=pl.BlockSpec((None, 2, None, nblk, NSA_DIM), lambda b, j: (b, 0, j, 0, 0)),
        out_shape=jax.ShapeDtypeStruct((batch, 2, g, nblk, NSA_DIM), F32),
        compiler_params=_cparams(("parallel", "parallel")),
        name="nsa_compress",
    )(proj, proj, pe_k, pe_v, w1_k, w1_v, w2_k, w2_v)


def _nsa_kernel(q_ref, kc_ref, vc_ref, ks_ref, vs_ref, kw_ref, vw_ref, gl_ref, o_ref,
                ksb_scr, vst_scr, kwb_scr, vwt_scr, selb_scr, s_scr, p_scr, sw_scr, pw_scr, accs_scr, accw_scr, *, tq, seq):
    grp = pl.program_id(1)
    qi = pl.program_id(2)
    nj = NSA_GROUP
    tk = ATT_TK
    n_diag = tq // tk
    n_sel = seq // SLC_BLOCK
    ncmp_pad = seq // CMP_STRIDE

    @pl.when(qi == 0)
    def _():
        _stage_kv(ks_ref, vs_ref, ksb_scr, vst_scr, seq, tk)
        _stage_kv(kw_ref, vw_ref, kwb_scr, vwt_scr, seq, tk)

    qall = jnp.concatenate([q_ref[:, j * NSA_DIM:(j + 1) * NSA_DIM] for j in range(nj)], axis=0)
    qall = (qall * (NSA_DIM ** -0.5 * LOG2E)).astype(BF16)
    tpos = qi * tq + lax.broadcasted_iota(jnp.int32, (1, tq), 1)

    def rep(x):
        return jnp.concatenate([x] * nj, axis=1)

    sc = lax.dot_general(kc_ref[...].astype(BF16), qall, _NT, preferred_element_type=F32)
    n_end = lax.broadcasted_iota(jnp.int32, (ncmp_pad, 1), 0) * CMP_STRIDE + (CMP_LEN - 1)
    cvis = rep(n_end <= tpos)
    m = jnp.max(jnp.where(cvis, sc, NEG), axis=0, keepdims=True)
    p = jnp.exp2(jnp.where(cvis, sc - m, NEG))
    p = p / jnp.maximum(jnp.sum(p, axis=0, keepdims=True), 1e-30)
    o_cmp = jnp.dot(vc_ref[...].T.astype(BF16), p.astype(BF16), preferred_element_type=F32)
    psum = p[:, 0:tq] + p[:, tq:2 * tq] + p[:, 2 * tq:3 * tq]
    cn = lax.broadcasted_iota(jnp.int32, (n_sel, ncmp_pad), 1) * CMP_STRIDE
    sb = lax.broadcasted_iota(jnp.int32, (n_sel, ncmp_pad), 0) * SLC_BLOCK
    overlap_t = jnp.where((cn < sb + SLC_BLOCK) & (cn + CMP_LEN > sb), 1.0, 0.0).astype(BF16)
    hi = psum.astype(BF16)
    r1 = psum - hi.astype(F32)
    mid = r1.astype(BF16)
    lo = (r1 - mid.astype(F32)).astype(BF16)
    imp = (jnp.dot(overlap_t, hi, preferred_element_type=F32)
           + jnp.dot(overlap_t, mid, preferred_element_type=F32)
           + jnp.dot(overlap_t, lo, preferred_element_type=F32))

    blk = lax.broadcasted_iota(jnp.int32, (n_sel, 1), 0)
    cur = tpos // SLC_BLOCK
    forced = (blk == 0) | (blk == cur) | (blk == cur - 1)
    valid = blk * SLC_BLOCK <= tpos
    score = jnp.where(valid, imp + jnp.where(forced, FORCE_BONUS, 0.0), -jnp.inf)
    rank = jnp.zeros((n_sel, tq), F32)
    for i in range(n_sel):
        ci = score[i:i + 1, :]
        rank += jnp.where(blk > i, jnp.where(ci >= score, 1.0, 0.0), jnp.where(ci > score, 1.0, 0.0))
    sel_mask = jnp.where(rank < float(min(SLC_TOPK, n_sel)), 0.0, NEG)
    for r in range(n_sel):
        selb_scr[r * SLC_BLOCK:(r + 1) * SLC_BLOCK, :] = jnp.broadcast_to(sel_mask[r:r + 1, :], (SLC_BLOCK, tq))

    kio = lax.broadcasted_iota(jnp.int32, (tk, 1), 0)

    def sel_bias(kb):
        return selb_scr[pl.ds(pl.multiple_of(kb * tk, tk), tk), :]

    _, l_s = _attend_range(qall, ksb_scr, vst_scr, 0, (qi + 1) * n_diag, lambda b: rep(sel_bias(b)),
                           lambda b: rep(jnp.where(b * tk + kio <= tpos, sel_bias(b), NEG)), n_diag,
                           s_scr, p_scr, accs_scr, tk)
    o_slc = accs_scr[...] / jnp.maximum(l_s, 1e-30)

    def win_bias(b):
        dist = tpos - (b * tk + kio)
        return rep(jnp.where((dist >= 0) & (dist < WINDOW), 0.0, NEG))

    m_w, l_w = _attend_range(qall, kwb_scr, vwt_scr, jnp.maximum(qi * n_diag - WINDOW // tk, 0),
                             (qi + 1) * n_diag, win_bias, win_bias, 1, sw_scr, pw_scr, accw_scr, tk)
    n_pad = rep(jnp.maximum(WINDOW - 1 - tpos, 0)).astype(F32)
    m_p = jnp.where(n_pad > 0.0, jnp.maximum(m_w, 0.0), m_w)
    a_p = jnp.exp2(m_w - m_p)
    l_p = jnp.where(n_pad > 0.0, n_pad * jnp.exp2(-jnp.maximum(m_p, 0.0)), 0.0)
    o_win = (accw_scr[...] * a_p) / jnp.maximum(l_w * a_p + l_p, 1e-30)

    gate = jax.nn.sigmoid(gl_ref[...].T)
    for j in range(nj):
        def grow(br):
            c0 = 3 * j + br
            c1 = 3 * (nj + j) + br
            return jnp.where(grp == 0, gate[c0:c0 + 1, :], gate[c1:c1 + 1, :])
        lanes = slice(j * tq, (j + 1) * tq)
        y_t = grow(0) * o_cmp[:, lanes] + grow(1) * o_slc[:, lanes] + grow(2) * o_win[:, lanes]
        o_ref[:, j * NSA_DIM:(j + 1) * NSA_DIM] = y_t.T.astype(BF16)


def _nsa(proj, gate_logits, cmp, batch, seq, tq=256):
    nq = seq // tq
    g = NSA_KV_HEADS
    gw = NSA_GROUP * NSA_DIM
    ncmp_pad = seq // CMP_STRIDE

    def kv(cb):
        return pl.BlockSpec((seq, LANE), lambda b, j, i: (b, cb + j))

    def cmp_spec(which):
        return pl.BlockSpec((None, None, None, ncmp_pad, NSA_DIM), lambda b, j, i: (b, which, j, 0, 0))

    kb_scr = pltpu.VMEM((seq, LANE), BF16)
    vt_scr = pltpu.VMEM((NSA_DIM, seq), BF16)
    acc_scr = pltpu.VMEM((NSA_DIM, NSA_GROUP * tq), F32)
    s_scr = pltpu.VMEM((ATT_TK, NSA_GROUP * tq), F32)
    p_scr = pltpu.VMEM((ATT_TK, NSA_GROUP * tq), BF16)
    return pl.pallas_call(
        functools.partial(_nsa_kernel, tq=tq, seq=seq),
        grid=(batch, g, nq),
        in_specs=[pl.BlockSpec((tq, gw), lambda b, j, i: (b * nq + i, CB_CQ * LANE // gw + j)),
                  cmp_spec(0), cmp_spec(1),
                  kv(CB_KS), kv(CB_VS), kv(CB_KW), kv(CB_VW),
                  pl.BlockSpec((tq, LANE), lambda b, j, i: (b * nq + i, 0))],
        out_specs=pl.BlockSpec((tq, gw), lambda b, j, i: (b * nq + i, j)),
        out_shape=jax.ShapeDtypeStruct((batch * seq, D_C), BF16),
        scratch_shapes=[kb_scr, vt_scr, kb_scr, vt_scr, pltpu.VMEM((seq, tq), F32),
                        s_scr, p_scr, s_scr, p_scr, acc_scr, acc_scr],
        compiler_params=_cparams(("parallel", "parallel", "arbitrary")),
        name="nsa",
    )(proj, cmp, cmp, proj, proj, proj, proj, gate_logits)


def kernel(x, attn_norm, w_in, da_lam_q1, da_lam_k1, da_lam_q2, da_lam_k2, da_norm, hg_gamma, hg_norm,
           nsa_pe_k, nsa_pe_v, nsa_ck_w1, nsa_ck_w2, nsa_cv_w1, nsa_cv_w2, w_out, ffn_norm,
           w_gate, w_up, w_down, final_norm):
    batch, seq, d = x.shape
    depth = w_in.shape[0]
    assert (CB_CQ * LANE) % (NSA_GROUP * NSA_DIM) == 0 and D_IN_MAIN + NSA_HEADS * 3 == D_IN
    xf = x.reshape(batch * seq, d)
    w_rows = _w_in_rows(w_in)
    for l in range(depth):
        proj, gate_logits = _norm_matmul(xf, attn_norm[l], w_rows, l)
        ya = _diff_attn(proj, da_lam_q1[l], da_lam_k1[l], da_lam_q2[l], da_lam_k2[l], da_norm[l],
                        l, batch, seq)
        yb = _hgrn(proj, hg_gamma, hg_norm[l], l, batch, seq)
        cmp = _compress(proj, nsa_pe_k, nsa_pe_v, nsa_ck_w1, nsa_cv_w1, nsa_ck_w2, nsa_cv_w2, l, batch, seq)
        yc = _nsa(proj, gate_logits, cmp, batch, seq)
        xf = _out_proj(ya, yb, yc, w_out, xf, l)
        u = _ffn_up(xf, ffn_norm[l], w_gate, w_up, l)
        xf = _matmul_res(u, w_down, xf, l)
    return _final_norm(xf, final_norm).reshape(batch, seq, d)
```

```python
import functools
import math

import jax
import jax.numpy as jnp
import numpy as np
from jax import lax
from jax.experimental import pallas as pl
from jax.experimental.pallas import tpu as pltpu

F32 = jnp.float32
BF16 = jnp.bfloat16

D_MODEL = 2048
DA_HEADS = 4
DA_QK_DIM = 64
DA_V_DIM = 128
HG_HEADS = 6
HG_DIM = 128
NSA_HEADS = 6
NSA_KV_HEADS = 2
NSA_GROUP = NSA_HEADS // NSA_KV_HEADS
NSA_DIM = 128
CMP_LEN = 32
CMP_STRIDE = 16
CMP_HIDDEN = 256
SLC_BLOCK = 64
SLC_TOPK = 16
WINDOW = 512
FORCE_BONUS = 1.0e4
D_A = DA_HEADS * DA_V_DIM
D_B = HG_HEADS * HG_DIM
D_C = NSA_HEADS * NSA_DIM
D_FF = ((8 * D_MODEL // 3 + 255) // 256) * 256
D_IN = 3 * D_A + 4 * D_B + D_C + 6 * NSA_KV_HEADS * NSA_DIM + NSA_HEADS * 3

LANE = 128
CB_AQ, CB_AK, CB_AV = 0, 4, 8
CB_BF, CB_BQ, CB_BI, CB_BG = 12, 18, 24, 30
CB_CQ = 36
CB_KC, CB_VC, CB_KS, CB_VS, CB_KW, CB_VW = 42, 44, 46, 48, 50, 52
D_IN_MAIN = 54 * LANE
D_IN_PAD = D_IN_MAIN

EPS = 1e-6
NEG = -1e30
LOG2E = 1.4426950408889634
VMEM_LIMIT = 56 * 1024 * 1024

_NT = (((1,), (1,)), ((), ()))


def _cparams(sem):
    return pltpu.CompilerParams(dimension_semantics=sem, vmem_limit_bytes=VMEM_LIMIT)


def _rms_rows(x, gain):
    return x * lax.rsqrt(jnp.mean(x * x, axis=-1, keepdims=True) + EPS) * gain


W_IN_ROWS = 256
D_IN_ROWS = -(-D_IN // W_IN_ROWS) * W_IN_ROWS


def _w_in_rows_kernel(w_ref, o_ref, *, depth, nchunk):
    rows = o_ref.shape[1]
    n_id = pl.program_id(0) * rows + lax.broadcasted_iota(jnp.int32, (rows, 1), 0)
    keep = n_id < D_IN
    by_chunk = jnp.swapaxes(w_ref[...], 0, 1)
    for c in range(nchunk):
        for l in range(depth):
            o_ref[l, :, c * LANE:(c + 1) * LANE] = jnp.where(keep, by_chunk[c * depth + l], 0.0).astype(BF16)


def _w_in_rows(w_in):
    depth, k, n = w_in.shape
    nchunk = k // LANE
    flat = w_in.reshape(depth, nchunk, LANE, n).transpose(3, 1, 0, 2).reshape(n, nchunk * depth, LANE)
    return pl.pallas_call(
        functools.partial(_w_in_rows_kernel, depth=depth, nchunk=nchunk),
        grid=(D_IN_ROWS // W_IN_ROWS,),
        in_specs=[pl.BlockSpec((W_IN_ROWS, nchunk * depth, LANE), lambda i: (i, 0, 0))],
        out_specs=pl.BlockSpec((depth, W_IN_ROWS, k), lambda i: (0, i, 0)),
        out_shape=jax.ShapeDtypeStruct((depth, D_IN_ROWS, k), BF16),
        compiler_params=_cparams(("parallel",)),
        name="w_in_rows",
    )(flat)


def _norm_matmul_kernel(x_ref, g_ref, w_ref, wt_ref, o_ref, ot_ref, h_scr):
    @pl.when(pl.program_id(1) == 0)
    def _():
        h = _rms_rows(x_ref[...], g_ref[...]).astype(BF16)
        h_scr[...] = h
        ot_ref[...] = lax.dot_general(h, wt_ref[...], _NT, preferred_element_type=F32)

    o_ref[...] = lax.dot_general(h_scr[...], w_ref[...], _NT, preferred_element_type=F32)


def _norm_matmul(x, gain, w_rows, layer, tm=1024, tn=768):
    m, k = x.shape
    return pl.pallas_call(
        _norm_matmul_kernel,
        grid=(m // tm, D_IN_MAIN // tn),
        in_specs=[pl.BlockSpec((tm, k), lambda i, j: (i, 0)),
                  pl.BlockSpec((1, k), lambda i, j: (0, 0)),
                  pl.BlockSpec((None, tn, k), lambda i, j: (layer, j, 0)),
                  pl.BlockSpec((None, LANE, k), lambda i, j: (layer, D_IN_MAIN // LANE, 0))],
        out_specs=[pl.BlockSpec((tm, tn), lambda i, j: (i, j)),
                   pl.BlockSpec((tm, LANE), lambda i, j: (i, 0))],
        out_shape=[jax.ShapeDtypeStruct((m, D_IN_MAIN), F32), jax.ShapeDtypeStruct((m, LANE), F32)],
        scratch_shapes=[pltpu.VMEM((tm, k), BF16)],
        compiler_params=_cparams(("parallel", "arbitrary")),
        name="norm_in_proj",
    )(x, gain.reshape(1, k), w_rows, w_rows)


def _ffn_up_kernel(x_ref, g_ref, wg_ref, wu_ref, wd_ref, o_ref, wdb_ref, h_scr):
    @pl.when(pl.program_id(1) == 0)
    def _():
        h_scr[...] = _rms_rows(x_ref[...], g_ref[...]).astype(BF16)

    h = h_scr[...]
    a = jnp.dot(h, wg_ref[...].astype(BF16), preferred_element_type=F32)
    b = jnp.dot(h, wu_ref[...].astype(BF16), preferred_element_type=F32)
    o_ref[...] = (a * jax.nn.sigmoid(a) * b).astype(BF16)
    wdb_ref[...] = wd_ref[...].astype(BF16)


def _ffn_up(x, gain, wg, wu, wd, layer, tm=1024, tn=512):
    m, k = x.shape
    n = wg.shape[2]
    nj = n // tn
    kd, nd = wd.shape[1:]
    slab = kd // ((m // tm) * nj)
    assert slab * (m // tm) * nj == kd and slab % 16 == 0
    return pl.pallas_call(
        _ffn_up_kernel,
        grid=(m // tm, nj),
        in_specs=[pl.BlockSpec((tm, k), lambda i, j: (i, 0)),
                  pl.BlockSpec((1, k), lambda i, j: (0, 0)),
                  pl.BlockSpec((None, k, tn), lambda i, j: (layer, 0, j)),
                  pl.BlockSpec((None, k, tn), lambda i, j: (layer, 0, j)),
                  pl.BlockSpec((None, slab, nd), lambda i, j: (layer, i * nj + j, 0))],
        out_specs=[pl.BlockSpec((tm, tn), lambda i, j: (i, j)),
                   pl.BlockSpec((slab, nd), lambda i, j: (i * nj + j, 0))],
        out_shape=[jax.ShapeDtypeStruct((m, n), BF16), jax.ShapeDtypeStruct((kd, nd), BF16)],
        scratch_shapes=[pltpu.VMEM((tm, k), BF16)],
        compiler_params=_cparams(("parallel", "arbitrary")),
        name="ffn_up",
    )(x, gain.reshape(1, k), wg, wu, wd)


def _matmul_res_kernel(a_ref, w_ref, r_ref, o_ref):
    o_ref[...] = r_ref[...] + jnp.dot(a_ref[...], w_ref[...], preferred_element_type=F32)


def _matmul_res(a, w, res, tm=1024, tn=512):
    m, k = a.shape
    n = w.shape[1]
    return pl.pallas_call(
        _matmul_res_kernel,
        grid=(m // tm, n // tn),
        in_specs=[pl.BlockSpec((tm, k), lambda i, j: (i, 0)),
                  pl.BlockSpec((k, tn), lambda i, j: (0, j)),
                  pl.BlockSpec((tm, tn), lambda i, j: (i, j))],
        out_specs=pl.BlockSpec((tm, tn), lambda i, j: (i, j)),
        out_shape=jax.ShapeDtypeStruct((m, n), F32),
        compiler_params=_cparams(("parallel", "arbitrary")),
        name="ffn_down",
    )(a, w, res)


def _out_proj_kernel(ya_ref, yb_ref, yc_ref, w_ref, r_ref, o_ref):
    acc = jnp.dot(ya_ref[...], w_ref[0:D_A, :].astype(BF16), preferred_element_type=F32)
    acc += jnp.dot(yb_ref[...], w_ref[D_A:D_A + D_B, :].astype(BF16), preferred_element_type=F32)
    acc += jnp.dot(yc_ref[...], w_ref[D_A + D_B:, :].astype(BF16), preferred_element_type=F32)
    o_ref[...] = r_ref[...] + acc


def _out_proj(ya, yb, yc, w, res, layer, tm=2048, tn=512):
    m = ya.shape[0]
    _, k, n = w.shape
    return pl.pallas_call(
        _out_proj_kernel,
        grid=(m // tm, n // tn),
        in_specs=[pl.BlockSpec((tm, D_A), lambda i, j: (i, 0)),
                  pl.BlockSpec((tm, D_B), lambda i, j: (i, 0)),
                  pl.BlockSpec((tm, D_C), lambda i, j: (i, 0)),
                  pl.BlockSpec((None, k, tn), lambda i, j: (layer, 0, j)),
                  pl.BlockSpec((tm, tn), lambda i, j: (i, j))],
        out_specs=pl.BlockSpec((tm, tn), lambda i, j: (i, j)),
        out_shape=jax.ShapeDtypeStruct((m, n), F32),
        compiler_params=_cparams(("parallel", "arbitrary")),
        name="out_proj",
    )(ya, yb, yc, w, res)


def _final_norm_kernel(x_ref, g_ref, o_ref):
    o_ref[...] = _rms_rows(x_ref[...], g_ref[...])


def _final_norm(x, gain, tm=512):
    m, k = x.shape
    return pl.pallas_call(
        _final_norm_kernel,
        grid=(m // tm,),
        in_specs=[pl.BlockSpec((tm, k), lambda i: (i, 0)),
                  pl.BlockSpec((1, k), lambda i: (0, 0))],
        out_specs=pl.BlockSpec((tm, k), lambda i: (i, 0)),
        out_shape=jax.ShapeDtypeStruct((m, k), F32),
        compiler_params=_cparams(("parallel",)),
        name="final_norm",
    )(x, gain.reshape(1, k))


ATT_TK = 256


def _softmax_block(s, bias, m, l):
    if bias is not None:
        s = s + bias
    m_new = jnp.maximum(m, jnp.max(s, axis=0, keepdims=True))
    alpha = jnp.exp2(m - m_new)
    p = jnp.exp2(s - m_new)
    return p.astype(BF16), alpha, m_new, alpha * l + jnp.sum(p, axis=0, keepdims=True)


def _attend_range(qall, kb_scr, vt_scr, lo, hi, bias_fn, tail_bias_fn, n_tail, s_scr, p_scr, acc_scr, tk):
    rows = qall.shape[0]

    def scores(b):
        k0 = pl.multiple_of(b * tk, tk)
        return lax.dot_general(kb_scr[pl.ds(k0, tk), :], qall, _NT, preferred_element_type=F32)

    def values(b, p):
        k0 = pl.multiple_of(b * tk, tk)
        return jnp.dot(vt_scr[:, pl.ds(k0, tk)], p, preferred_element_type=F32)

    def stage(b, s_cur, p_prev, m, l, alpha_prev, bias, want_next):
        pv_prev = values(jnp.maximum(b - 1, lo), p_prev)
        s_next = scores(b + 1) if want_next else None
        p, alpha, m, l = _softmax_block(s_cur, bias, m, l)
        acc_scr[...] = alpha_prev * acc_scr[...] + pv_prev
        return s_next, p, m, l, alpha

    acc_scr[...] = jnp.zeros_like(acc_scr)
    s_scr[...] = scores(lo)
    p_scr[...] = jnp.zeros(p_scr.shape, BF16)
    n_main = hi - n_tail - lo
    odd = n_main % 2

    def one(i, carry):
        m, l, alpha_prev = carry
        b = lo + i
        s_next, p, m, l, alpha = stage(b, s_scr[...], p_scr[...], m, l, alpha_prev, bias_fn(b), True)
        s_scr[...] = s_next
        p_scr[...] = p
        return m, l, alpha

    def two(i, carry):
        m, l, alpha_prev = carry
        b = lo + odd + 2 * i
        s_mid, p_mid, m, l, alpha = stage(b, s_scr[...], p_scr[...], m, l, alpha_prev, bias_fn(b), True)
        s_next, p, m, l, alpha = stage(b + 1, s_mid, p_mid, m, l, alpha, bias_fn(b + 1), True)
        s_scr[...] = s_next
        p_scr[...] = p
        return m, l, alpha

    carry = (jnp.full((1, rows), NEG, F32), jnp.zeros((1, rows), F32), jnp.ones((1, rows), F32))
    carry = lax.fori_loop(0, odd, one, carry)
    m, l, alpha_prev = lax.fori_loop(0, n_main // 2, two, carry)
    s_cur = s_scr[...]
    p_prev = p_scr[...]
    for t in range(n_tail):
        b = hi - n_tail + t
        s_cur, p_prev, m, l, alpha_prev = stage(b, s_cur, p_prev, m, l, alpha_prev, tail_bias_fn(b),
                                                t + 1 < n_tail)
    acc_scr[...] = alpha_prev * acc_scr[...] + values(hi - 1, p_prev)
    return m, l


def _stage_kv(k_ref, v_ref, kb_scr, vt_scr, seq, tk):
    for c in range(seq // tk):
        rows = slice(c * tk, (c + 1) * tk)
        kb_scr[rows, :] = k_ref[rows, :].astype(BF16)
        vt_scr[:, rows] = v_ref[rows, :].T.astype(BF16)


def _diff_attn_kernel(q_ref, k_ref, v_ref, lq1_ref, lk1_ref, lq2_ref, lk2_ref, gn_ref, o_ref,
                      kb_scr, vt_scr, s_scr, p_scr, acc_scr, *, lam_init, tq, seq):
    h = pl.program_id(1)
    i = pl.program_id(2)
    tk = ATT_TK
    n_diag = tq // tk

    @pl.when(i == 0)
    def _():
        _stage_kv(k_ref, v_ref, kb_scr, vt_scr, seq, tk)

    q = q_ref[...] * (DA_QK_DIM ** -0.5 * LOG2E)
    lane = lax.broadcasted_iota(jnp.int32, q.shape, 1)
    qbd = jnp.concatenate([jnp.where(lane < DA_QK_DIM, q, 0.0), jnp.where(lane >= DA_QK_DIM, q, 0.0)],
                          axis=0).astype(BF16)
    tpos = i * tq + lax.broadcasted_iota(jnp.int32, (1, tq), 1)
    kio = lax.broadcasted_iota(jnp.int32, (tk, 1), 0)

    def causal(b):
        keep = jnp.where(b * tk + kio <= tpos, 0.0, NEG)
        return jnp.concatenate([keep, keep], axis=1)

    _, l = _attend_range(qbd, kb_scr, vt_scr, 0, (i + 1) * n_diag, lambda b: None, causal, n_diag,
                         s_scr, p_scr, acc_scr, tk)
    o_t = acc_scr[...] / jnp.maximum(l, 1e-30)
    lam = (jnp.exp(jnp.sum(lq1_ref[...] * lk1_ref[...], axis=-1, keepdims=True))
           - jnp.exp(jnp.sum(lq2_ref[...] * lk2_ref[...], axis=-1, keepdims=True)) + lam_init)
    o = (o_t[:, :tq] - lam * o_t[:, tq:]).T
    y = _rms_rows(o, gn_ref[pl.ds(h, 1), :]) * (1.0 - lam_init)
    o_ref[...] = y.astype(BF16)


def _diff_attn(proj, lq1, lk1, lq2, lk2, gn, layer, batch, seq, tq=512):
    nq = seq // tq
    lam_init = 0.8 - 0.6 * math.exp(-0.3 * layer)
    vec = pl.BlockSpec((1, DA_QK_DIM), lambda b, h, i: (0, 0))
    return pl.pallas_call(
        functools.partial(_diff_attn_kernel, lam_init=lam_init, tq=tq, seq=seq),
        grid=(batch, DA_HEADS, nq),
        in_specs=[pl.BlockSpec((tq, LANE), lambda b, h, i: (b * nq + i, CB_AQ + h)),
                  pl.BlockSpec((seq, LANE), lambda b, h, i: (b, CB_AK + h)),
                  pl.BlockSpec((seq, LANE), lambda b, h, i: (b, CB_AV + h)),
                  vec, vec, vec, vec,
                  pl.BlockSpec((DA_HEADS, DA_V_DIM), lambda b, h, i: (0, 0))],
        out_specs=pl.BlockSpec((tq, LANE), lambda b, h, i: (b * nq + i, h)),
        out_shape=jax.ShapeDtypeStruct((batch * seq, D_A), BF16),
        scratch_shapes=[pltpu.VMEM((seq, LANE), BF16), pltpu.VMEM((DA_V_DIM, seq), BF16),
                        pltpu.VMEM((ATT_TK, 2 * tq), F32), pltpu.VMEM((ATT_TK, 2 * tq), BF16),
                        pltpu.VMEM((DA_V_DIM, 2 * tq), F32)],
        compiler_params=_cparams(("parallel", "parallel", "arbitrary")),
        name="diff_attn",
    )(proj, proj, proj, lq1.reshape(1, -1), lk1.reshape(1, -1), lq2.reshape(1, -1), lk2.reshape(1, -1), gn)


HG_LEVELS = (64, 32, 16, 8, 4, 2, 1)
HG_SUB = 2 * HG_LEVELS[0]


def _hgrn_consts():
    tb = HG_SUB
    t = np.arange(tb)[:, None]
    u = np.arange(tb)[None, :]
    sums, owns = [], []
    for c in HG_LEVELS:
        mid = (t // (2 * c)) * (2 * c) + c
        second = (t % (2 * c)) >= c
        sums.append(np.where(second, (u >= mid) & (u <= t), (u > t) & (u < mid)))
        owns.append(((t // (2 * c)) == (u // (2 * c))) & second & ((u % (2 * c)) < c))
    sums.append(u <= t)
    return (jnp.asarray(np.concatenate(sums, 0).astype(np.float32), BF16),
            jnp.asarray(np.stack(owns).astype(np.float32)))


def _hgrn_kernel(f_ref, q_ref, i_ref, g_ref, gam_ref, gn_ref, sums_ref, owns_ref, o_ref, st_scr,
                 *, layer, ts):
    h = pl.program_id(1)
    tb = HG_SUB
    nlev = len(HG_LEVELS)

    @pl.when(pl.program_id(2) == 0)
    def _():
        st_scr[...] = jnp.zeros_like(st_scr)

    if layer > 0:
        gam = gam_ref[...]
        e = jnp.exp(gam - jnp.max(gam, axis=0, keepdims=True))
        lb = jnp.sum(e[1:layer + 1], axis=0, keepdims=True) / jnp.sum(e, axis=0, keepdims=True)
        log_lb = jnp.log(lb)
        log_1mlb = jnp.log(1.0 - lb)
    gain = gn_ref[pl.ds(h, 1), :]
    st = st_scr[...]

    for sb in range(ts // tb):
        rows = slice(sb * tb, (sb + 1) * tb)
        z = f_ref[rows, :]
        soft = jnp.log(1.0 + jnp.exp(-jnp.abs(z)))
        logsig = jnp.minimum(z, 0.0) - soft
        logsig_neg = jnp.minimum(-z, 0.0) - soft
        if layer == 0:
            logf = logsig
            key = jnp.exp(logsig_neg)
        else:
            b = log_1mlb + logsig
            logf = jnp.maximum(log_lb, b) + jnp.log(1.0 + jnp.exp(-jnp.abs(log_lb - b)))
            key = (1.0 - lb) * jnp.exp(logsig_neg)
        logf2 = logf * LOG2E
        hi = logf2.astype(BF16)
        mid = (logf2 - hi.astype(F32)).astype(BF16)
        parts = jnp.dot(sums_ref[...], jnp.concatenate([hi, mid], axis=1), preferred_element_type=F32)

        def expo(i):
            blk = parts[i * tb:(i + 1) * tb]
            return blk[:, 0:HG_DIM] + blk[:, HG_DIM:2 * HG_DIM]

        q = q_ref[rows, :]
        v = i_ref[rows, :]
        vb = v.astype(BF16)
        scores = None
        for i in range(nlev):
            w = jnp.exp2(expo(i))
            a = lax.dot_general((q * w).astype(BF16), (key * w).astype(BF16), _NT, preferred_element_type=F32)
            a = a * owns_ref[i]
            scores = a if scores is None else scores + a
        cum = expo(nlev)
        o = jnp.dot(scores.astype(BF16), vb, preferred_element_type=F32)
        o = o + jnp.sum(q * key, axis=-1, keepdims=True) * v
        o = o + lax.dot_general((q * jnp.exp2(cum)).astype(BF16), st.astype(BF16), _NT, preferred_element_type=F32)
        gt = g_ref[rows, :]
        o_ref[rows, :] = (_rms_rows(o, gain) * (gt * jax.nn.sigmoid(gt))).astype(BF16)
        last = cum[tb - 1:tb, :]
        ke = (key * jnp.exp2(last - cum)).astype(BF16)
        st = st * jnp.exp2(last) + jnp.dot(v.T.astype(BF16), ke, preferred_element_type=F32)

    st_scr[...] = st


def _hgrn(proj, gamma, gn, layer, batch, seq, ts=2048):
    nt = seq // ts
    depth = gamma.shape[0]
    sums, owns = _hgrn_consts()

    def blk(cb):
        return pl.BlockSpec((ts, LANE), lambda b, h, t: (b * nt + t, cb + h))

    return pl.pallas_call(
        functools.partial(_hgrn_kernel, layer=layer, ts=ts),
        grid=(batch, HG_HEADS, nt),
        in_specs=[blk(CB_BF), blk(CB_BQ), blk(CB_BI), blk(CB_BG),
                  pl.BlockSpec((depth, LANE), lambda b, h, t: (0, h)),
                  pl.BlockSpec((HG_HEADS, HG_DIM), lambda b, h, t: (0, 0)),
                  pl.BlockSpec(sums.shape, lambda b, h, t: (0, 0)),
                  pl.BlockSpec(owns.shape, lambda b, h, t: (0, 0, 0))],
        out_specs=pl.BlockSpec((ts, LANE), lambda b, h, t: (b * nt + t, h)),
        out_shape=jax.ShapeDtypeStruct((batch * seq, D_B), BF16),
        scratch_shapes=[pltpu.VMEM((HG_DIM, HG_DIM), F32)],
        compiler_params=_cparams(("parallel", "parallel", "arbitrary")),
        name="hgrn2",
    )(proj, proj, proj, proj, gamma, gn, sums, owns)


def _gelu_tanh(x):
    return 0.5 * x * (1.0 + jnp.tanh(math.sqrt(2.0 / math.pi) * (x + 0.044715 * (x * x * x))))


def _compress_kernel(xk_ref, xv_ref, pek_ref, pev_ref, w1k_ref, w1v_ref, w2k_ref, w2v_ref, o_ref, *, nblk):
    half = CMP_LEN // 2
    for which, (x_ref, pe_ref, w1_ref, w2_ref) in enumerate(((xk_ref, pek_ref, w1k_ref, w2k_ref),
                                                             (xv_ref, pev_ref, w1v_ref, w2v_ref))):
        top = jnp.zeros((nblk, CMP_HIDDEN), F32)
        bot = jnp.zeros((nblk, CMP_HIDDEN), F32)
        for r in range(half):
            xr = x_ref[pl.ds(r, nblk, stride=CMP_STRIDE), :]
            top += jnp.dot((xr + pe_ref[r:r + 1, :]).astype(BF16),
                           w1_ref[r * NSA_DIM:(r + 1) * NSA_DIM, :].astype(BF16), preferred_element_type=F32)
            bot += jnp.dot((xr + pe_ref[half + r:half + r + 1, :]).astype(BF16),
                           w1_ref[(half + r) * NSA_DIM:(half + r + 1) * NSA_DIM, :].astype(BF16),
                           preferred_element_type=F32)
        pre = top + pltpu.roll(bot, nblk - 1, 0)
        o_ref[which] = jnp.dot(_gelu_tanh(pre).astype(BF16), w2_ref[...].astype(BF16),
                               preferred_element_type=F32)


def _compress(proj, pe_k, pe_v, w1_k, w1_v, w2_k, w2_v, layer, batch, seq):
    nblk = seq // CMP_STRIDE
    g = NSA_KV_HEADS

    def per_layer(*shape):
        return pl.BlockSpec((None,) + shape, lambda b, j: (layer,) + (0,) * len(shape))

    return pl.pallas_call(
        functools.partial(_compress_kernel, nblk=nblk),
        grid=(batch, g),
        in_specs=[pl.BlockSpec((seq, LANE), lambda b, j: (b, CB_KC + j)),
                  pl.BlockSpec((seq, LANE), lambda b, j: (b, CB_VC + j)),
                  per_layer(CMP_LEN, NSA_DIM), per_layer(CMP_LEN, NSA_DIM),
                  per_layer(CMP_LEN * NSA_DIM, CMP_HIDDEN), per_layer(CMP_LEN * NSA_DIM, CMP_HIDDEN),
                  per_layer(CMP_HIDDEN, NSA_DIM), per_layer(CMP_HIDDEN, NSA_DIM)],
        out_specs=pl.BlockSpec((None, 2, None, nblk, NSA_DIM), lambda b, j: (b, 0, j, 0, 0)),
        out_shape=jax.ShapeDtypeStruct((batch, 2, g, nblk, NSA_DIM), F32),
        compiler_params=_cparams(("parallel", "parallel")),
        name="nsa_compress",
    )(proj, proj, pe_k, pe_v, w1_k, w1_v, w2_k, w2_v)


def _nsa_kernel(q_ref, kc_ref, vc_ref, ks_ref, vs_ref, kw_ref, vw_ref, gl_ref, o_ref,
                ksb_scr, vst_scr, kwb_scr, vwt_scr, selb_scr, s_scr, p_scr, sw_scr, pw_scr, accs_scr, accw_scr, *, tq, seq):
    grp = pl.program_id(1)
    qi = pl.program_id(2)
    nj = NSA_GROUP
    tk = ATT_TK
    n_diag = tq // tk
    n_sel = seq // SLC_BLOCK
    ncmp_pad = seq // CMP_STRIDE

    @pl.when(qi == 0)
    def _():
        _stage_kv(ks_ref, vs_ref, ksb_scr, vst_scr, seq, tk)
        _stage_kv(kw_ref, vw_ref, kwb_scr, vwt_scr, seq, tk)

    qall = jnp.concatenate([q_ref[:, j * NSA_DIM:(j + 1) * NSA_DIM] for j in range(nj)], axis=0)
    qall = (qall * (NSA_DIM ** -0.5 * LOG2E)).astype(BF16)
    tpos = qi * tq + lax.broadcasted_iota(jnp.int32, (1, tq), 1)

    def rep(x):
        return jnp.concatenate([x] * nj, axis=1)

    sc = lax.dot_general(kc_ref[...].astype(BF16), qall, _NT, preferred_element_type=F32)
    n_end = lax.broadcasted_iota(jnp.int32, (ncmp_pad, 1), 0) * CMP_STRIDE + (CMP_LEN - 1)
    cvis = rep(n_end <= tpos)
    m = jnp.max(jnp.where(cvis, sc, NEG), axis=0, keepdims=True)
    p = jnp.exp2(jnp.where(cvis, sc - m, NEG))
    p = p / jnp.maximum(jnp.sum(p, axis=0, keepdims=True), 1e-30)
    o_cmp = jnp.dot(vc_ref[...].T.astype(BF16), p.astype(BF16), preferred_element_type=F32)
    psum = p[:, 0:tq] + p[:, tq:2 * tq] + p[:, 2 * tq:3 * tq]
    cn = lax.broadcasted_iota(jnp.int32, (n_sel, ncmp_pad), 1) * CMP_STRIDE
    sb = lax.broadcasted_iota(jnp.int32, (n_sel, ncmp_pad), 0) * SLC_BLOCK
    overlap_t = jnp.where((cn < sb + SLC_BLOCK) & (cn + CMP_LEN > sb), 1.0, 0.0).astype(BF16)
    hi = psum.astype(BF16)
    r1 = psum - hi.astype(F32)
    mid = r1.astype(BF16)
    lo = (r1 - mid.astype(F32)).astype(BF16)
    imp = (jnp.dot(overlap_t, hi, preferred_element_type=F32)
           + jnp.dot(overlap_t, mid, preferred_element_type=F32)
           + jnp.dot(overlap_t, lo, preferred_element_type=F32))

    blk = lax.broadcasted_iota(jnp.int32, (n_sel, 1), 0)
    cur = tpos // SLC_BLOCK
    forced = (blk == 0) | (blk == cur) | (blk == cur - 1)
    valid = blk * SLC_BLOCK <= tpos
    score = jnp.where(valid, imp + jnp.where(forced, FORCE_BONUS, 0.0), -jnp.inf)
    rank = jnp.zeros((n_sel, tq), F32)
    for i in range(n_sel):
        ci = score[i:i + 1, :]
        rank += jnp.where(blk > i, jnp.where(ci >= score, 1.0, 0.0), jnp.where(ci > score, 1.0, 0.0))
    sel_mask = jnp.where(rank < float(min(SLC_TOPK, n_sel)), 0.0, NEG)
    for r in range(n_sel):
        selb_scr[r * SLC_BLOCK:(r + 1) * SLC_BLOCK, :] = jnp.broadcast_to(sel_mask[r:r + 1, :], (SLC_BLOCK, tq))

    kio = lax.broadcasted_iota(jnp.int32, (tk, 1), 0)

    def sel_bias(kb):
        return selb_scr[pl.ds(pl.multiple_of(kb * tk, tk), tk), :]

    _, l_s = _attend_range(qall, ksb_scr, vst_scr, 0, (qi + 1) * n_diag, lambda b: rep(sel_bias(b)),
                           lambda b: rep(jnp.where(b * tk + kio <= tpos, sel_bias(b), NEG)), n_diag,
                           s_scr, p_scr, accs_scr, tk)
    o_slc = accs_scr[...] / jnp.maximum(l_s, 1e-30)

    def win_bias(b):
        dist = tpos - (b * tk + kio)
        return rep(jnp.where((dist >= 0) & (dist < WINDOW), 0.0, NEG))

    m_w, l_w = _attend_range(qall, kwb_scr, vwt_scr, jnp.maximum(qi * n_diag - WINDOW // tk, 0),
                             (qi + 1) * n_diag, win_bias, win_bias, 1, sw_scr, pw_scr, accw_scr, tk)
    n_pad = rep(jnp.maximum(WINDOW - 1 - tpos, 0)).astype(F32)
    m_p = jnp.where(n_pad > 0.0, jnp.maximum(m_w, 0.0), m_w)
    a_p = jnp.exp2(m_w - m_p)
    l_p = jnp.where(n_pad > 0.0, n_pad * jnp.exp2(-jnp.maximum(m_p, 0.0)), 0.0)
    o_win = (accw_scr[...] * a_p) / jnp.maximum(l_w * a_p + l_p, 1e-30)

    gate = jax.nn.sigmoid(gl_ref[...].T)
    for j in range(nj):
        def grow(br):
            c0 = 3 * j + br
            c1 = 3 * (nj + j) + br
            return jnp.where(grp == 0, gate[c0:c0 + 1, :], gate[c1:c1 + 1, :])
        lanes = slice(j * tq, (j + 1) * tq)
        y_t = grow(0) * o_cmp[:, lanes] + grow(1) * o_slc[:, lanes] + grow(2) * o_win[:, lanes]
        o_ref[:, j * NSA_DIM:(j + 1) * NSA_DIM] = y_t.T.astype(BF16)


def _nsa(proj, gate_logits, cmp, batch, seq, tq=256):
    nq = seq // tq
    g = NSA_KV_HEADS
    gw = NSA_GROUP * NSA_DIM
    ncmp_pad = seq // CMP_STRIDE

    def kv(cb):
        return pl.BlockSpec((seq, LANE), lambda b, j, i: (b, cb + j))

    def cmp_spec(which):
        return pl.BlockSpec((None, None, None, ncmp_pad, NSA_DIM), lambda b, j, i: (b, which, j, 0, 0))

    kb_scr = pltpu.VMEM((seq, LANE), BF16)
    vt_scr = pltpu.VMEM((NSA_DIM, seq), BF16)
    acc_scr = pltpu.VMEM((NSA_DIM, NSA_GROUP * tq), F32)
    s_scr = pltpu.VMEM((ATT_TK, NSA_GROUP * tq), F32)
    p_scr = pltpu.VMEM((ATT_TK, NSA_GROUP * tq), BF16)
    return pl.pallas_call(
        functools.partial(_nsa_kernel, tq=tq, seq=seq),
        grid=(batch, g, nq),
        in_specs=[pl.BlockSpec((tq, gw), lambda b, j, i: (b * nq + i, CB_CQ * LANE // gw + j)),
                  cmp_spec(0), cmp_spec(1),
                  kv(CB_KS), kv(CB_VS), kv(CB_KW), kv(CB_VW),
                  pl.BlockSpec((tq, LANE), lambda b, j, i: (b * nq + i, 0))],
        out_specs=pl.BlockSpec((tq, gw), lambda b, j, i: (b * nq + i, j)),
        out_shape=jax.ShapeDtypeStruct((batch * seq, D_C), BF16),
        scratch_shapes=[kb_scr, vt_scr, kb_scr, vt_scr, pltpu.VMEM((seq, tq), F32),
                        s_scr, p_scr, s_scr, p_scr, acc_scr, acc_scr],
        compiler_params=_cparams(("parallel", "parallel", "arbitrary")),
        name="nsa",
    )(proj, cmp, cmp, proj, proj, proj, proj, gate_logits)


def kernel(x, attn_norm, w_in, da_lam_q1, da_lam_k1, da_lam_q2, da_lam_k2, da_norm, hg_gamma, hg_norm,
           nsa_pe_k, nsa_pe_v, nsa_ck_w1, nsa_ck_w2, nsa_cv_w1, nsa_cv_w2, w_out, ffn_norm,
           w_gate, w_up, w_down, final_norm):
    batch, seq, d = x.shape
    depth = w_in.shape[0]
    assert (CB_CQ * LANE) % (NSA_GROUP * NSA_DIM) == 0 and D_IN_MAIN + NSA_HEADS * 3 == D_IN
    xf = x.reshape(batch * seq, d)
    w_rows = _w_in_rows(w_in)
    for l in range(depth):
        proj, gate_logits = _norm_matmul(xf, attn_norm[l], w_rows, l)
        ya = _diff_attn(proj, da_lam_q1[l], da_lam_k1[l], da_lam_q2[l], da_lam_k2[l], da_norm[l],
                        l, batch, seq)
        yb = _hgrn(proj, hg_gamma, hg_norm[l], l, batch, seq)
        cmp = _compress(proj, nsa_pe_k, nsa_pe_v, nsa_ck_w1, nsa_cv_w1, nsa_ck_w2, nsa_cv_w2, l, batch, seq)
        yc = _nsa(proj, gate_logits, cmp, batch, seq)
        xf = _out_proj(ya, yb, yc, w_out, xf, l)
        u, w_down_b = _ffn_up(xf, ffn_norm[l], w_gate, w_up, w_down, l)
        xf = _matmul_res(u, w_down_b, xf)
    return _final_norm(xf, final_norm).reshape(batch, seq, d)
```

```python
import functools
import math

import jax
import jax.numpy as jnp
import numpy as np
from jax import lax
from jax.experimental import pallas as pl
from jax.experimental.pallas import tpu as pltpu

F32 = jnp.float32
BF16 = jnp.bfloat16

D_MODEL = 2048
DA_HEADS = 4
DA_QK_DIM = 64
DA_V_DIM = 128
HG_HEADS = 6
HG_DIM = 128
NSA_HEADS = 6
NSA_KV_HEADS = 2
NSA_GROUP = NSA_HEADS // NSA_KV_HEADS
NSA_DIM = 128
CMP_LEN = 32
CMP_STRIDE = 16
CMP_HIDDEN = 256
SLC_BLOCK = 64
SLC_TOPK = 16
WINDOW = 512
FORCE_BONUS = 1.0e4
D_A = DA_HEADS * DA_V_DIM
D_B = HG_HEADS * HG_DIM
D_C = NSA_HEADS * NSA_DIM
D_FF = ((8 * D_MODEL // 3 + 255) // 256) * 256
D_IN = 3 * D_A + 4 * D_B + D_C + 6 * NSA_KV_HEADS * NSA_DIM + NSA_HEADS * 3

LANE = 128
CB_AQ, CB_AK, CB_AV = 0, 4, 8
CB_BF, CB_BQ, CB_BI, CB_BG = 12, 18, 24, 30
CB_CQ = 36
CB_KC, CB_VC, CB_KS, CB_VS, CB_KW, CB_VW = 42, 44, 46, 48, 50, 52
D_IN_MAIN = 54 * LANE
D_IN_PAD = D_IN_MAIN

EPS = 1e-6
NEG = -1e30
LOG2E = 1.4426950408889634
VMEM_LIMIT = 56 * 1024 * 1024

_NT = (((1,), (1,)), ((), ()))


def _cparams(sem):
    return pltpu.CompilerParams(dimension_semantics=sem, vmem_limit_bytes=VMEM_LIMIT)


def _rms_rows(x, gain):
    return x * lax.rsqrt(jnp.mean(x * x, axis=-1, keepdims=True) + EPS) * gain


W_IN_ROWS = 256
D_IN_ROWS = -(-D_IN // W_IN_ROWS) * W_IN_ROWS


def _w_in_rows_kernel(w_ref, o_ref, *, depth, nchunk):
    rows = o_ref.shape[1]
    n_id = pl.program_id(0) * rows + lax.broadcasted_iota(jnp.int32, (rows, 1), 0)
    keep = n_id < D_IN
    by_chunk = jnp.swapaxes(w_ref[...], 0, 1)
    for c in range(nchunk):
        for l in range(depth):
            o_ref[l, :, c * LANE:(c + 1) * LANE] = jnp.where(keep, by_chunk[c * depth + l], 0.0).astype(BF16)


def _w_in_rows(w_in):
    depth, k, n = w_in.shape
    nchunk = k // LANE
    flat = w_in.reshape(depth, nchunk, LANE, n).transpose(3, 1, 0, 2).reshape(n, nchunk * depth, LANE)
    return pl.pallas_call(
        functools.partial(_w_in_rows_kernel, depth=depth, nchunk=nchunk),
        grid=(D_IN_ROWS // W_IN_ROWS,),
        in_specs=[pl.BlockSpec((W_IN_ROWS, nchunk * depth, LANE), lambda i: (i, 0, 0))],
        out_specs=pl.BlockSpec((depth, W_IN_ROWS, k), lambda i: (0, i, 0)),
        out_shape=jax.ShapeDtypeStruct((depth, D_IN_ROWS, k), BF16),
        compiler_params=_cparams(("parallel",)),
        name="w_in_rows",
    )(flat)


def _norm_matmul_kernel(x_ref, g_ref, w_ref, wt_ref, o_ref, ot_ref, h_scr):
    @pl.when(pl.program_id(1) == 0)
    def _():
        h = _rms_rows(x_ref[...], g_ref[...]).astype(BF16)
        h_scr[...] = h
        ot_ref[...] = lax.dot_general(h, wt_ref[...], _NT, preferred_element_type=F32)

    o_ref[...] = lax.dot_general(h_scr[...], w_ref[...], _NT, preferred_element_type=F32)


def _norm_matmul(x, gain, w_rows, layer, tm=1024, tn=768):
    m, k = x.shape
    return pl.pallas_call(
        _norm_matmul_kernel,
        grid=(m // tm, D_IN_MAIN // tn),
        in_specs=[pl.BlockSpec((tm, k), lambda i, j: (i, 0)),
                  pl.BlockSpec((1, k), lambda i, j: (0, 0)),
                  pl.BlockSpec((None, tn, k), lambda i, j: (layer, j, 0)),
                  pl.BlockSpec((None, LANE, k), lambda i, j: (layer, D_IN_MAIN // LANE, 0))],
        out_specs=[pl.BlockSpec((tm, tn), lambda i, j: (i, j)),
                   pl.BlockSpec((tm, LANE), lambda i, j: (i, 0))],
        out_shape=[jax.ShapeDtypeStruct((m, D_IN_MAIN), F32), jax.ShapeDtypeStruct((m, LANE), F32)],
        scratch_shapes=[pltpu.VMEM((tm, k), BF16)],
        compiler_params=_cparams(("parallel", "arbitrary")),
        name="norm_in_proj",
    )(x, gain.reshape(1, k), w_rows, w_rows)


W_RING = 3


def _ffn_up_kernel(x_ref, g_ref, wg_hbm, wu_hbm, wd_ref, o_ref, wdb_ref, h_scr, wg_buf, wu_buf, sem,
                   *, layer, tn):
    i = pl.program_id(0)
    j = pl.program_id(1)
    nj = pl.num_programs(1)
    step = i * nj + j
    last = pl.num_programs(0) * nj - 1

    def copies(s):
        col = pl.multiple_of((s % nj) * tn, tn)
        slot = s % W_RING
        return (pltpu.make_async_copy(wg_hbm.at[layer, :, pl.ds(col, tn)], wg_buf.at[slot], sem.at[0, slot]),
                pltpu.make_async_copy(wu_hbm.at[layer, :, pl.ds(col, tn)], wu_buf.at[slot], sem.at[1, slot]))

    def start(s):
        for c in copies(s):
            c.start()

    @pl.when(step == 0)
    def _():
        start(step)
        start(step + 1)

    @pl.when(step + 2 <= last)
    def _():
        start(step + 2)

    @pl.when(j == 0)
    def _():
        h_scr[...] = _rms_rows(x_ref[...], g_ref[...]).astype(BF16)

    for c in copies(step):
        c.wait()
    slot = step % W_RING
    h = h_scr[...]
    a = jnp.dot(h, wg_buf[slot].astype(BF16), preferred_element_type=F32)
    b = jnp.dot(h, wu_buf[slot].astype(BF16), preferred_element_type=F32)
    o_ref[...] = (a * jax.nn.sigmoid(a) * b).astype(BF16)
    wdb_ref[...] = wd_ref[...].astype(BF16)


def _ffn_up(x, gain, wg, wu, wd, layer, tm=1024, tn=512):
    m, k = x.shape
    n = wg.shape[2]
    nj = n // tn
    kd, nd = wd.shape[1:]
    slab = kd // ((m // tm) * nj)
    assert slab * (m // tm) * nj == kd and slab % 16 == 0 and (m // tm) * nj >= 2
    return pl.pallas_call(
        functools.partial(_ffn_up_kernel, layer=layer, tn=tn),
        grid=(m // tm, nj),
        in_specs=[pl.BlockSpec((tm, k), lambda i, j: (i, 0)),
                  pl.BlockSpec((1, k), lambda i, j: (0, 0)),
                  pl.BlockSpec(memory_space=pl.ANY),
                  pl.BlockSpec(memory_space=pl.ANY),
                  pl.BlockSpec((None, slab, nd), lambda i, j: (layer, i * nj + j, 0))],
        out_specs=[pl.BlockSpec((tm, tn), lambda i, j: (i, j)),
                   pl.BlockSpec((slab, nd), lambda i, j: (i * nj + j, 0))],
        out_shape=[jax.ShapeDtypeStruct((m, n), BF16), jax.ShapeDtypeStruct((kd, nd), BF16)],
        scratch_shapes=[pltpu.VMEM((tm, k), BF16), pltpu.VMEM((W_RING, k, tn), F32),
                        pltpu.VMEM((W_RING, k, tn), F32), pltpu.SemaphoreType.DMA((2, W_RING))],
        compiler_params=_cparams(("arbitrary", "arbitrary")),
        name="ffn_up",
    )(x, gain.reshape(1, k), wg, wu, wd)


def _matmul_res_kernel(a_ref, w_ref, r_ref, o_ref):
    o_ref[...] = r_ref[...] + jnp.dot(a_ref[...], w_ref[...], preferred_element_type=F32)


def _matmul_res(a, w, res, tm=1024, tn=512):
    m, k = a.shape
    n = w.shape[1]
    return pl.pallas_call(
        _matmul_res_kernel,
        grid=(m // tm, n // tn),
        in_specs=[pl.BlockSpec((tm, k), lambda i, j: (i, 0)),
                  pl.BlockSpec((k, tn), lambda i, j: (0, j)),
                  pl.BlockSpec((tm, tn), lambda i, j: (i, j))],
        out_specs=pl.BlockSpec((tm, tn), lambda i, j: (i, j)),
        out_shape=jax.ShapeDtypeStruct((m, n), F32),
        compiler_params=_cparams(("parallel", "arbitrary")),
        name="ffn_down",
    )(a, w, res)


def _out_proj_kernel(ya_ref, yb_ref, yc_ref, w_ref, r_ref, o_ref):
    acc = jnp.dot(ya_ref[...], w_ref[0:D_A, :].astype(BF16), preferred_element_type=F32)
    acc += jnp.dot(yb_ref[...], w_ref[D_A:D_A + D_B, :].astype(BF16), preferred_element_type=F32)
    acc += jnp.dot(yc_ref[...], w_ref[D_A + D_B:, :].astype(BF16), preferred_element_type=F32)
    o_ref[...] = r_ref[...] + acc


def _out_proj(ya, yb, yc, w, res, layer, tm=2048, tn=512):
    m = ya.shape[0]
    _, k, n = w.shape
    return pl.pallas_call(
        _out_proj_kernel,
        grid=(m // tm, n // tn),
        in_specs=[pl.BlockSpec((tm, D_A), lambda i, j: (i, 0)),
                  pl.BlockSpec((tm, D_B), lambda i, j: (i, 0)),
                  pl.BlockSpec((tm, D_C), lambda i, j: (i, 0)),
                  pl.BlockSpec((None, k, tn), lambda i, j: (layer, 0, j)),
                  pl.BlockSpec((tm, tn), lambda i, j: (i, j))],
        out_specs=pl.BlockSpec((tm, tn), lambda i, j: (i, j)),
        out_shape=jax.ShapeDtypeStruct((m, n), F32),
        compiler_params=_cparams(("parallel", "arbitrary")),
        name="out_proj",
    )(ya, yb, yc, w, res)


def _final_norm_kernel(x_ref, g_ref, o_ref):
    o_ref[...] = _rms_rows(x_ref[...], g_ref[...])


def _final_norm(x, gain, tm=512):
    m, k = x.shape
    return pl.pallas_call(
        _final_norm_kernel,
        grid=(m // tm,),
        in_specs=[pl.BlockSpec((tm, k), lambda i: (i, 0)),
                  pl.BlockSpec((1, k), lambda i: (0, 0))],
        out_specs=pl.BlockSpec((tm, k), lambda i: (i, 0)),
        out_shape=jax.ShapeDtypeStruct((m, k), F32),
        compiler_params=_cparams(("parallel",)),
        name="final_norm",
    )(x, gain.reshape(1, k))


ATT_TK = 256


def _softmax_block(s, bias, m, l):
    if bias is not None:
        s = s + bias
    m_new = jnp.maximum(m, jnp.max(s, axis=0, keepdims=True))
    alpha = jnp.exp2(m - m_new)
    p = jnp.exp2(s - m_new)
    return p.astype(BF16), alpha, m_new, alpha * l + jnp.sum(p, axis=0, keepdims=True)


def _attend_range(qall, kb_scr, vt_scr, lo, hi, bias_fn, tail_bias_fn, n_tail, s_scr, p_scr, acc_scr, tk):
    rows = qall.shape[0]

    def scores(b):
        k0 = pl.multiple_of(b * tk, tk)
        return lax.dot_general(kb_scr[pl.ds(k0, tk), :], qall, _NT, preferred_element_type=F32)

    def values(b, p):
        k0 = pl.multiple_of(b * tk, tk)
        return jnp.dot(vt_scr[:, pl.ds(k0, tk)], p, preferred_element_type=F32)

    def stage(b, s_cur, p_prev, m, l, alpha_prev, bias, want_next):
        pv_prev = values(jnp.maximum(b - 1, lo), p_prev)
        s_next = scores(b + 1) if want_next else None
        p, alpha, m, l = _softmax_block(s_cur, bias, m, l)
        acc_scr[...] = alpha_prev * acc_scr[...] + pv_prev
        return s_next, p, m, l, alpha

    acc_scr[...] = jnp.zeros_like(acc_scr)
    s_scr[...] = scores(lo)
    p_scr[...] = jnp.zeros(p_scr.shape, BF16)
    n_main = hi - n_tail - lo
    odd = n_main % 2

    def one(i, carry):
        m, l, alpha_prev = carry
        b = lo + i
        s_next, p, m, l, alpha = stage(b, s_scr[...], p_scr[...], m, l, alpha_prev, bias_fn(b), True)
        s_scr[...] = s_next
        p_scr[...] = p
        return m, l, alpha

    def two(i, carry):
        m, l, alpha_prev = carry
        b = lo + odd + 2 * i
        s_mid, p_mid, m, l, alpha = stage(b, s_scr[...], p_scr[...], m, l, alpha_prev, bias_fn(b), True)
        s_next, p, m, l, alpha = stage(b + 1, s_mid, p_mid, m, l, alpha, bias_fn(b + 1), True)
        s_scr[...] = s_next
        p_scr[...] = p
        return m, l, alpha

    carry = (jnp.full((1, rows), NEG, F32), jnp.zeros((1, rows), F32), jnp.ones((1, rows), F32))
    carry = lax.fori_loop(0, odd, one, carry)
    m, l, alpha_prev = lax.fori_loop(0, n_main // 2, two, carry)
    s_cur = s_scr[...]
    p_prev = p_scr[...]
    for t in range(n_tail):
        b = hi - n_tail + t
        s_cur, p_prev, m, l, alpha_prev = stage(b, s_cur, p_prev, m, l, alpha_prev, tail_bias_fn(b),
                                                t + 1 < n_tail)
    acc_scr[...] = alpha_prev * acc_scr[...] + values(hi - 1, p_prev)
    return m, l


def _stage_kv(k_ref, v_ref, kb_scr, vt_scr, seq, tk):
    for c in range(seq // tk):
        rows = slice(c * tk, (c + 1) * tk)
        kb_scr[rows, :] = k_ref[rows, :].astype(BF16)
        vt_scr[:, rows] = v_ref[rows, :].T.astype(BF16)


def _diff_attn_kernel(q_ref, k_ref, v_ref, lq1_ref, lk1_ref, lq2_ref, lk2_ref, gn_ref, o_ref,
                      kb_scr, vt_scr, s_scr, p_scr, acc_scr, *, lam_init, tq, seq):
    h = pl.program_id(1)
    i = pl.program_id(2)
    tk = ATT_TK
    n_diag = tq // tk

    @pl.when(i == 0)
    def _():
        _stage_kv(k_ref, v_ref, kb_scr, vt_scr, seq, tk)

    q = q_ref[...] * (DA_QK_DIM ** -0.5 * LOG2E)
    lane = lax.broadcasted_iota(jnp.int32, q.shape, 1)
    qbd = jnp.concatenate([jnp.where(lane < DA_QK_DIM, q, 0.0), jnp.where(lane >= DA_QK_DIM, q, 0.0)],
                          axis=0).astype(BF16)
    tpos = i * tq + lax.broadcasted_iota(jnp.int32, (1, tq), 1)
    kio = lax.broadcasted_iota(jnp.int32, (tk, 1), 0)

    def causal(b):
        keep = jnp.where(b * tk + kio <= tpos, 0.0, NEG)
        return jnp.concatenate([keep, keep], axis=1)

    _, l = _attend_range(qbd, kb_scr, vt_scr, 0, (i + 1) * n_diag, lambda b: None, causal, n_diag,
                         s_scr, p_scr, acc_scr, tk)
    o_t = acc_scr[...] / jnp.maximum(l, 1e-30)
    lam = (jnp.exp(jnp.sum(lq1_ref[...] * lk1_ref[...], axis=-1, keepdims=True))
           - jnp.exp(jnp.sum(lq2_ref[...] * lk2_ref[...], axis=-1, keepdims=True)) + lam_init)
    o = (o_t[:, :tq] - lam * o_t[:, tq:]).T
    y = _rms_rows(o, gn_ref[pl.ds(h, 1), :]) * (1.0 - lam_init)
    o_ref[...] = y.astype(BF16)


def _diff_attn(proj, lq1, lk1, lq2, lk2, gn, layer, batch, seq, tq=512):
    nq = seq // tq
    lam_init = 0.8 - 0.6 * math.exp(-0.3 * layer)
    vec = pl.BlockSpec((1, DA_QK_DIM), lambda b, h, i: (0, 0))
    return pl.pallas_call(
        functools.partial(_diff_attn_kernel, lam_init=lam_init, tq=tq, seq=seq),
        grid=(batch, DA_HEADS, nq),
        in_specs=[pl.BlockSpec((tq, LANE), lambda b, h, i: (b * nq + i, CB_AQ + h)),
                  pl.BlockSpec((seq, LANE), lambda b, h, i: (b, CB_AK + h)),
                  pl.BlockSpec((seq, LANE), lambda b, h, i: (b, CB_AV + h)),
                  vec, vec, vec, vec,
                  pl.BlockSpec((DA_HEADS, DA_V_DIM), lambda b, h, i: (0, 0))],
        out_specs=pl.BlockSpec((tq, LANE), lambda b, h, i: (b * nq + i, h)),
        out_shape=jax.ShapeDtypeStruct((batch * seq, D_A), BF16),
        scratch_shapes=[pltpu.VMEM((seq, LANE), BF16), pltpu.VMEM((DA_V_DIM, seq), BF16),
                        pltpu.VMEM((ATT_TK, 2 * tq), F32), pltpu.VMEM((ATT_TK, 2 * tq), BF16),
                        pltpu.VMEM((DA_V_DIM, 2 * tq), F32)],
        compiler_params=_cparams(("parallel", "parallel", "arbitrary")),
        name="diff_attn",
    )(proj, proj, proj, lq1.reshape(1, -1), lk1.reshape(1, -1), lq2.reshape(1, -1), lk2.reshape(1, -1), gn)


HG_LEVELS = (64, 32, 16, 8, 4, 2, 1)
HG_SUB = 2 * HG_LEVELS[0]


def _hgrn_consts():
    tb = HG_SUB
    t = np.arange(tb)[:, None]
    u = np.arange(tb)[None, :]
    sums, owns = [], []
    for c in HG_LEVELS:
        mid = (t // (2 * c)) * (2 * c) + c
        second = (t % (2 * c)) >= c
        sums.append(np.where(second, (u >= mid) & (u <= t), (u > t) & (u < mid)))
        owns.append(((t // (2 * c)) == (u // (2 * c))) & second & ((u % (2 * c)) < c))
    sums.append(u <= t)
    return (jnp.asarray(np.concatenate(sums, 0).astype(np.float32), BF16),
            jnp.asarray(np.stack(owns).astype(np.float32)))


def _hgrn_kernel(f_ref, q_ref, i_ref, g_ref, gam_ref, gn_ref, sums_ref, owns_ref, o_ref, st_scr,
                 *, layer, ts):
    h = pl.program_id(1)
    tb = HG_SUB
    nlev = len(HG_LEVELS)

    @pl.when(pl.program_id(2) == 0)
    def _():
        st_scr[...] = jnp.zeros_like(st_scr)

    if layer > 0:
        gam = gam_ref[...]
        e = jnp.exp(gam - jnp.max(gam, axis=0, keepdims=True))
        lb = jnp.sum(e[1:layer + 1], axis=0, keepdims=True) / jnp.sum(e, axis=0, keepdims=True)
        log_lb = jnp.log(lb)
        log_1mlb = jnp.log(1.0 - lb)
    gain = gn_ref[pl.ds(h, 1), :]
    st = st_scr[...]

    for sb in range(ts // tb):
        rows = slice(sb * tb, (sb + 1) * tb)
        z = f_ref[rows, :]
        soft = jnp.log(1.0 + jnp.exp(-jnp.abs(z)))
        logsig = jnp.minimum(z, 0.0) - soft
        logsig_neg = jnp.minimum(-z, 0.0) - soft
        if layer == 0:
            logf = logsig
            key = jnp.exp(logsig_neg)
        else:
            b = log_1mlb + logsig
            logf = jnp.maximum(log_lb, b) + jnp.log(1.0 + jnp.exp(-jnp.abs(log_lb - b)))
            key = (1.0 - lb) * jnp.exp(logsig_neg)
        logf2 = logf * LOG2E
        hi = logf2.astype(BF16)
        mid = (logf2 - hi.astype(F32)).astype(BF16)
        parts = jnp.dot(sums_ref[...], jnp.concatenate([hi, mid], axis=1), preferred_element_type=F32)

        def expo(i):
            blk = parts[i * tb:(i + 1) * tb]
            return blk[:, 0:HG_DIM] + blk[:, HG_DIM:2 * HG_DIM]

        q = q_ref[rows, :]
        v = i_ref[rows, :]
        vb = v.astype(BF16)
        scores = None
        for i in range(nlev):
            w = jnp.exp2(expo(i))
            a = lax.dot_general((q * w).astype(BF16), (key * w).astype(BF16), _NT, preferred_element_type=F32)
            a = a * owns_ref[i]
            scores = a if scores is None else scores + a
        cum = expo(nlev)
        o = jnp.dot(scores.astype(BF16), vb, preferred_element_type=F32)
        o = o + jnp.sum(q * key, axis=-1, keepdims=True) * v
        o = o + lax.dot_general((q * jnp.exp2(cum)).astype(BF16), st.astype(BF16), _NT, preferred_element_type=F32)
        gt = g_ref[rows, :]
        o_ref[rows, :] = (_rms_rows(o, gain) * (gt * jax.nn.sigmoid(gt))).astype(BF16)
        last = cum[tb - 1:tb, :]
        ke = (key * jnp.exp2(last - cum)).astype(BF16)
        st = st * jnp.exp2(last) + jnp.dot(v.T.astype(BF16), ke, preferred_element_type=F32)

    st_scr[...] = st


def _hgrn(proj, gamma, gn, layer, batch, seq, ts=2048):
    nt = seq // ts
    depth = gamma.shape[0]
    sums, owns = _hgrn_consts()

    def blk(cb):
        return pl.BlockSpec((ts, LANE), lambda b, h, t: (b * nt + t, cb + h))

    return pl.pallas_call(
        functools.partial(_hgrn_kernel, layer=layer, ts=ts),
        grid=(batch, HG_HEADS, nt),
        in_specs=[blk(CB_BF), blk(CB_BQ), blk(CB_BI), blk(CB_BG),
                  pl.BlockSpec((depth, LANE), lambda b, h, t: (0, h)),
                  pl.BlockSpec((HG_HEADS, HG_DIM), lambda b, h, t: (0, 0)),
                  pl.BlockSpec(sums.shape, lambda b, h, t: (0, 0)),
                  pl.BlockSpec(owns.shape, lambda b, h, t: (0, 0, 0))],
        out_specs=pl.BlockSpec((ts, LANE), lambda b, h, t: (b * nt + t, h)),
        out_shape=jax.ShapeDtypeStruct((batch * seq, D_B), BF16),
        scratch_shapes=[pltpu.VMEM((HG_DIM, HG_DIM), F32)],
        compiler_params=_cparams(("parallel", "parallel", "arbitrary")),
        name="hgrn2",
    )(proj, proj, proj, proj, gamma, gn, sums, owns)


def _gelu_tanh(x):
    return 0.5 * x * (1.0 + jnp.tanh(math.sqrt(2.0 / math.pi) * (x + 0.044715 * (x * x * x))))


def _compress_kernel(xk_ref, xv_ref, pek_ref, pev_ref, w1k_ref, w1v_ref, w2k_ref, w2v_ref, o_ref, *, nblk):
    half = CMP_LEN // 2
    for which, (x_ref, pe_ref, w1_ref, w2_ref) in enumerate(((xk_ref, pek_ref, w1k_ref, w2k_ref),
                                                             (xv_ref, pev_ref, w1v_ref, w2v_ref))):
        top = jnp.zeros((nblk, CMP_HIDDEN), F32)
        bot = jnp.zeros((nblk, CMP_HIDDEN), F32)
        for r in range(half):
            xr = x_ref[pl.ds(r, nblk, stride=CMP_STRIDE), :]
            top += jnp.dot((xr + pe_ref[r:r + 1, :]).astype(BF16),
                           w1_ref[r * NSA_DIM:(r + 1) * NSA_DIM, :].astype(BF16), preferred_element_type=F32)
            bot += jnp.dot((xr + pe_ref[half + r:half + r + 1, :]).astype(BF16),
                           w1_ref[(half + r) * NSA_DIM:(half + r + 1) * NSA_DIM, :].astype(BF16),
                           preferred_element_type=F32)
        pre = top + pltpu.roll(bot, nblk - 1, 0)
        o_ref[which] = jnp.dot(_gelu_tanh(pre).astype(BF16), w2_ref[...].astype(BF16),
                               preferred_element_type=F32)


def _compress(proj, pe_k, pe_v, w1_k, w1_v, w2_k, w2_v, layer, batch, seq):
    nblk = seq // CMP_STRIDE
    g = NSA_KV_HEADS

    def per_layer(*shape):
        return pl.BlockSpec((None,) + shape, lambda b, j: (layer,) + (0,) * len(shape))

    return pl.pallas_call(
        functools.partial(_compress_kernel, nblk=nblk),
        grid=(batch, g),
        in_specs=[pl.BlockSpec((seq, LANE), lambda b, j: (b, CB_KC + j)),
                  pl.BlockSpec((seq, LANE), lambda b, j: (b, CB_VC + j)),
                  per_layer(CMP_LEN, NSA_DIM), per_layer(CMP_LEN, NSA_DIM),
                  per_layer(CMP_LEN * NSA_DIM, CMP_HIDDEN), per_layer(CMP_LEN * NSA_DIM, CMP_HIDDEN),
                  per_layer(CMP_HIDDEN, NSA_DIM), per_layer(CMP_HIDDEN, NSA_DIM)],
        out_specs=pl.BlockSpec((None, 2, None, nblk, NSA_DIM), lambda b, j: (b, 0, j, 0, 0)),
        out_shape=jax.ShapeDtypeStruct((batch, 2, g, nblk, NSA_DIM), F32),
        compiler_params=_cparams(("parallel", "parallel")),
        name="nsa_compress",
    )(proj, proj, pe_k, pe_v, w1_k, w1_v, w2_k, w2_v)


def _nsa_kernel(q_ref, kc_ref, vc_ref, ks_ref, vs_ref, kw_ref, vw_ref, gl_ref, o_ref,
                ksb_scr, vst_scr, kwb_scr, vwt_scr, selb_scr, s_scr, p_scr, sw_scr, pw_scr, accs_scr, accw_scr, *, tq, seq):
    grp = pl.program_id(1)
    qi = pl.program_id(2)
    nj = NSA_GROUP
    tk = ATT_TK
    n_diag = tq // tk
    n_sel = seq // SLC_BLOCK
    ncmp_pad = seq // CMP_STRIDE

    @pl.when(qi == 0)
    def _():
        _stage_kv(ks_ref, vs_ref, ksb_scr, vst_scr, seq, tk)
        _stage_kv(kw_ref, vw_ref, kwb_scr, vwt_scr, seq, tk)

    qall = jnp.concatenate([q_ref[:, j * NSA_DIM:(j + 1) * NSA_DIM] for j in range(nj)], axis=0)
    qall = (qall * (NSA_DIM ** -0.5 * LOG2E)).astype(BF16)
    tpos = qi * tq + lax.broadcasted_iota(jnp.int32, (1, tq), 1)

    def rep(x):
        return jnp.concatenate([x] * nj, axis=1)

    sc = lax.dot_general(kc_ref[...].astype(BF16), qall, _NT, preferred_element_type=F32)
    n_end = lax.broadcasted_iota(jnp.int32, (ncmp_pad, 1), 0) * CMP_STRIDE + (CMP_LEN - 1)
    cvis = rep(n_end <= tpos)
    m = jnp.max(jnp.where(cvis, sc, NEG), axis=0, keepdims=True)
    p = jnp.exp2(jnp.where(cvis, sc - m, NEG))
    p = p / jnp.maximum(jnp.sum(p, axis=0, keepdims=True), 1e-30)
    o_cmp = jnp.dot(vc_ref[...].T.astype(BF16), p.astype(BF16), preferred_element_type=F32)
    psum = p[:, 0:tq] + p[:, tq:2 * tq] + p[:, 2 * tq:3 * tq]
    cn = lax.broadcasted_iota(jnp.int32, (n_sel, ncmp_pad), 1) * CMP_STRIDE
    sb = lax.broadcasted_iota(jnp.int32, (n_sel, ncmp_pad), 0) * SLC_BLOCK
    overlap_t = jnp.where((cn < sb + SLC_BLOCK) & (cn + CMP_LEN > sb), 1.0, 0.0).astype(BF16)
    hi = psum.astype(BF16)
    r1 = psum - hi.astype(F32)
    mid = r1.astype(BF16)
    lo = (r1 - mid.astype(F32)).astype(BF16)
    imp = (jnp.dot(overlap_t, hi, preferred_element_type=F32)
           + jnp.dot(overlap_t, mid, preferred_element_type=F32)
           + jnp.dot(overlap_t, lo, preferred_element_type=F32))

    blk = lax.broadcasted_iota(jnp.int32, (n_sel, 1), 0)
    cur = tpos // SLC_BLOCK
    forced = (blk == 0) | (blk == cur) | (blk == cur - 1)
    valid = blk * SLC_BLOCK <= tpos
    score = jnp.where(valid, imp + jnp.where(forced, FORCE_BONUS, 0.0), -jnp.inf)
    rank = jnp.zeros((n_sel, tq), F32)
    for i in range(n_sel):
        ci = score[i:i + 1, :]
        rank += jnp.where(blk > i, jnp.where(ci >= score, 1.0, 0.0), jnp.where(ci > score, 1.0, 0.0))
    sel_mask = jnp.where(rank < float(min(SLC_TOPK, n_sel)), 0.0, NEG)
    for r in range(n_sel):
        selb_scr[r * SLC_BLOCK:(r + 1) * SLC_BLOCK, :] = jnp.broadcast_to(sel_mask[r:r + 1, :], (SLC_BLOCK, tq))

    kio = lax.broadcasted_iota(jnp.int32, (tk, 1), 0)

    def sel_bias(kb):
        return selb_scr[pl.ds(pl.multiple_of(kb * tk, tk), tk), :]

    _, l_s = _attend_range(qall, ksb_scr, vst_scr, 0, (qi + 1) * n_diag, lambda b: rep(sel_bias(b)),
                           lambda b: rep(jnp.where(b * tk + kio <= tpos, sel_bias(b), NEG)), n_diag,
                           s_scr, p_scr, accs_scr, tk)
    o_slc = accs_scr[...] / jnp.maximum(l_s, 1e-30)

    def win_bias(b):
        dist = tpos - (b * tk + kio)
        return rep(jnp.where((dist >= 0) & (dist < WINDOW), 0.0, NEG))

    m_w, l_w = _attend_range(qall, kwb_scr, vwt_scr, jnp.maximum(qi * n_diag - WINDOW // tk, 0),
                             (qi + 1) * n_diag, win_bias, win_bias, 1, sw_scr, pw_scr, accw_scr, tk)
    n_pad = rep(jnp.maximum(WINDOW - 1 - tpos, 0)).astype(F32)
    m_p = jnp.where(n_pad > 0.0, jnp.maximum(m_w, 0.0), m_w)
    a_p = jnp.exp2(m_w - m_p)
    l_p = jnp.where(n_pad > 0.0, n_pad * jnp.exp2(-jnp.maximum(m_p, 0.0)), 0.0)
    o_win = (accw_scr[...] * a_p) / jnp.maximum(l_w * a_p + l_p, 1e-30)

    gate = jax.nn.sigmoid(gl_ref[...].T)
    for j in range(nj):
        def grow(br):
            c0 = 3 * j + br
            c1 = 3 * (nj + j) + br
            return jnp.where(grp == 0, gate[c0:c0 + 1, :], gate[c1:c1 + 1, :])
        lanes = slice(j * tq, (j + 1) * tq)
        y_t = grow(0) * o_cmp[:, lanes] + grow(1) * o_slc[:, lanes] + grow(2) * o_win[:, lanes]
        o_ref[:, j * NSA_DIM:(j + 1) * NSA_DIM] = y_t.T.astype(BF16)


def _nsa(proj, gate_logits, cmp, batch, seq, tq=256):
    nq = seq // tq
    g = NSA_KV_HEADS
    gw = NSA_GROUP * NSA_DIM
    ncmp_pad = seq // CMP_STRIDE

    def kv(cb):
        return pl.BlockSpec((seq, LANE), lambda b, j, i: (b, cb + j))

    def cmp_spec(which):
        return pl.BlockSpec((None, None, None, ncmp_pad, NSA_DIM), lambda b, j, i: (b, which, j, 0, 0))

    kb_scr = pltpu.VMEM((seq, LANE), BF16)
    vt_scr = pltpu.VMEM((NSA_DIM, seq), BF16)
    acc_scr = pltpu.VMEM((NSA_DIM, NSA_GROUP * tq), F32)
    s_scr = pltpu.VMEM((ATT_TK, NSA_GROUP * tq), F32)
    p_scr = pltpu.VMEM((ATT_TK, NSA_GROUP * tq), BF16)
    return pl.pallas_call(
        functools.partial(_nsa_kernel, tq=tq, seq=seq),
        grid=(batch, g, nq),
        in_specs=[pl.BlockSpec((tq, gw), lambda b, j, i: (b * nq + i, CB_CQ * LANE // gw + j)),
                  cmp_spec(0), cmp_spec(1),
                  kv(CB_KS), kv(CB_VS), kv(CB_KW), kv(CB_VW),
                  pl.BlockSpec((tq, LANE), lambda b, j, i: (b * nq + i, 0))],
        out_specs=pl.BlockSpec((tq, gw), lambda b, j, i: (b * nq + i, j)),
        out_shape=jax.ShapeDtypeStruct((batch * seq, D_C), BF16),
        scratch_shapes=[kb_scr, vt_scr, kb_scr, vt_scr, pltpu.VMEM((seq, tq), F32),
                        s_scr, p_scr, s_scr, p_scr, acc_scr, acc_scr],
        compiler_params=_cparams(("parallel", "parallel", "arbitrary")),
        name="nsa",
    )(proj, cmp, cmp, proj, proj, proj, proj, gate_logits)


def kernel(x, attn_norm, w_in, da_lam_q1, da_lam_k1, da_lam_q2, da_lam_k2, da_norm, hg_gamma, hg_norm,
           nsa_pe_k, nsa_pe_v, nsa_ck_w1, nsa_ck_w2, nsa_cv_w1, nsa_cv_w2, w_out, ffn_norm,
           w_gate, w_up, w_down, final_norm):
    batch, seq, d = x.shape
    depth = w_in.shape[0]
    assert (CB_CQ * LANE) % (NSA_GROUP * NSA_DIM) == 0 and D_IN_MAIN + NSA_HEADS * 3 == D_IN
    xf = x.reshape(batch * seq, d)
    w_rows = _w_in_rows(w_in)
    for l in range(depth):
        proj, gate_logits = _norm_matmul(xf, attn_norm[l], w_rows, l)
        ya = _diff_attn(proj, da_lam_q1[l], da_lam_k1[l], da_lam_q2[l], da_lam_k2[l], da_norm[l],
                        l, batch, seq)
        yb = _hgrn(proj, hg_gamma, hg_norm[l], l, batch, seq)
        cmp = _compress(proj, nsa_pe_k, nsa_pe_v, nsa_ck_w1, nsa_cv_w1, nsa_ck_w2, nsa_cv_w2, l, batch, seq)
        yc = _nsa(proj, gate_logits, cmp, batch, seq)
        xf = _out_proj(ya, yb, yc, w_out, xf, l)
        u, w_down_b = _ffn_up(xf, ffn_norm[l], w_gate, w_up, w_down, l)
        xf = _matmul_res(u, w_down_b, xf)
    return _final_norm(xf, final_norm).reshape(batch, seq, d)
```
